```python
import jax, jax.numpy as jnp
from jax import lax
import numpy as np

D_MODEL = 2048
BATCH = 8
SEQ = 4096
DEPTH = 4
DEC_BATCH = 4
DEC_SEQ = 2048
PAST_LEN = 128

GRID_W = 64
ATT_WIDTH = D_MODEL // 2
RWKV_WIDTH = D_MODEL - ATT_WIDTH
ATT_HEAD_DIM = 128
ATT_HEADS = ATT_WIDTH // ATT_HEAD_DIM
RWKV_HEAD_DIM = 64
RWKV_HEADS = RWKV_WIDTH // RWKV_HEAD_DIM
NA_ROWS = 8
NA_COLS = 16
DECAY_LORA = 64
AAA_LORA = 64
MV_LORA = 32
GATE_LORA = 64
D_FF = 4 * D_MODEL
IN_WIDTH = 3 * ATT_WIDTH + 3 * RWKV_WIDTH
NORM_EPS = 1e-6
GN_EPS = 64e-5

kernel_name = "hymba_natten_rwkv7_bidir_encoder"


def rmsnorm(x, g):
    xf = x.astype(jnp.float32)
    y = xf * lax.rsqrt(jnp.mean(xf * xf, axis=-1, keepdims=True) + NORM_EPS)
    return (y * g.astype(jnp.float32)).astype(x.dtype)


def token_shift(u, mu):
    prev = jnp.pad(u[:, :-1], ((0, 0), (1, 0), (0, 0)))
    nxt = jnp.pad(u[:, 1:], ((0, 0), (0, 1), (0, 0)))
    return u + mu[0] * (prev - u) + mu[1] * (nxt - u)


def neighbourhood_attention(q, k, v, rpb):
    B, T, H, Dh = q.shape
    rows = T // GRID_W
    wr = min(NA_ROWS, rows)
    qg = q.reshape(B, rows, GRID_W, H, Dh)
    kg = k.reshape(B, rows, GRID_W, H, Dh)
    vg = v.reshape(B, rows, GRID_W, H, Dh)
    cols = jnp.arange(GRID_W)
    col_start = jnp.clip(cols - NA_COLS // 2, 0, GRID_W - NA_COLS)
    col_idx = col_start[:, None] + jnp.arange(NA_COLS)[None, :]
    dc = col_idx - cols[:, None]
    scale = ATT_HEAD_DIM ** -0.5

    def row_block(r):
        rs = jnp.clip(r - wr // 2, 0, rows - wr)
        q_r = lax.dynamic_index_in_dim(qg, r, axis=1, keepdims=False)
        k_w = lax.dynamic_slice_in_dim(kg, rs, wr, axis=1)
        v_w = lax.dynamic_slice_in_dim(vg, rs, wr, axis=1)
        k_sel = k_w[:, :, col_idx]
        v_sel = v_w[:, :, col_idx]
        dr = rs + jnp.arange(wr) - r
        bias = rpb[:, (dr + NA_ROWS - 1)[None, :, None], (dc + NA_COLS - 1)[:, None, :]]
        s = jnp.einsum('bqhd,bwqchd->bhqwc', q_r, k_sel).astype(jnp.float32) * scale
        s = s + bias.astype(jnp.float32)[None]
        p = jax.nn.softmax(s.reshape(B, H, GRID_W, wr * NA_COLS), axis=-1)
        p = p.reshape(B, H, GRID_W, wr, NA_COLS).astype(v.dtype)
        return jnp.einsum('bhqwc,bwqchd->bqhd', p, v_sel)

    out = lax.map(row_block, jnp.arange(rows))
    return jnp.moveaxis(out, 0, 1).reshape(B, T, H, Dh)


def wkv_scan(r, w, k, v, a, b, reverse):
    f32 = jnp.float32
    B, T, H, N = r.shape
    xs = tuple(jnp.moveaxis(u.astype(f32), 1, 0) for u in (r, w, k, v, a, b))

    def step(S, inp):
        r_t, w_t, k_t, v_t, a_t, b_t = inp
        sa = jnp.einsum('bhij,bhj->bhi', S, a_t)
        S = S * w_t[:, :, None, :] + sa[..., None] * b_t[:, :, None, :] + v_t[..., None] * k_t[:, :, None, :]
        return S, jnp.einsum('bhij,bhj->bhi', S, r_t)

    S0 = jnp.zeros((B, H, N, N), f32)
    _, y = lax.scan(step, S0, xs, reverse=reverse)
    return jnp.moveaxis(y, 0, 1)


def heads(u):
    return u.reshape(u.shape[:-1] + (RWKV_HEADS, RWKV_HEAD_DIM))


def rwkv_time_mix(h, r, k, v, v_first, mu_rkv, mu_x, w0, w1, w2, a0, a1, a2, g1, g2,
                  k_k, k_a, r_k, ln_w, ln_b, vres):
    f32 = jnp.float32
    B, T, _ = h.shape
    r = token_shift(r, mu_rkv[0])
    k = token_shift(k, mu_rkv[1])
    v = token_shift(v, mu_rkv[2])
    xw = token_shift(h, mu_x[0])
    xa = token_shift(h, mu_x[1])
    xg = token_shift(h, mu_x[2])
    if vres is None:
        v_first = v
    else:
        mu_v, v0, v1, v2 = vres
        xv = token_shift(h, mu_v)
        v = v + (v_first - v) * jax.nn.sigmoid(v0 + (xv @ v1) @ v2)
    w_log = w0[:, None, None, :] + jnp.einsum('ebtr,erc->ebtc', jnp.tanh(jnp.einsum('btd,edr->ebtr', xw, w1)), w2)
    decay = jnp.exp(-jnp.exp(-jax.nn.softplus(-w_log.astype(f32)) - 0.5))
    a = jax.nn.sigmoid(a0[:, None, None, :] + jnp.einsum('ebtr,erc->ebtc', jnp.einsum('btd,edr->ebtr', xa, a1), a2))
    g = jax.nn.sigmoid(xg @ g1) @ g2
    kk = heads((k * k_k).astype(f32))
    kk = kk * lax.rsqrt(jnp.maximum(jnp.sum(kk * kk, axis=-1, keepdims=True), 1e-24))
    k_dir = k[None] * (1.0 + (a - 1.0) * k_a)
    rh = heads(r)
    vh = heads(v)
    y = jnp.zeros((B, T, RWKV_HEADS, RWKV_HEAD_DIM), f32)
    for d, reverse in ((0, False), (1, True)):
        kd = heads(k_dir[d])
        ad = heads(a[d]).astype(f32)
        y = y + wkv_scan(rh, heads(decay[d]), kd, vh, -kk, kk * ad, reverse)
        y = y + (jnp.sum(rh * kd * r_k, axis=-1, keepdims=True) * vh).astype(f32)
    mean = jnp.mean(y, axis=-1, keepdims=True)
    var = jnp.mean(jnp.square(y - mean), axis=-1, keepdims=True)
    y = ((y - mean) * lax.rsqrt(var + GN_EPS)).reshape(B, T, RWKV_WIDTH)
    y = (y * ln_w.astype(f32) + ln_b.astype(f32)).astype(h.dtype)
    return y * g, v_first


def trunk(x, c, w_ada, b_ada, g_pre_mix, g_post_mix, g_pre_ffn, g_post_ffn, w_in, rpb,
          g_att_out, mu_rkv, mu_x, w0, w1, w2, a0, a1, a2, g1, g2, mu_v, v0, v1, v2,
          k_k, k_a, r_k, ln_x_w, ln_x_b, w_out, w_ffn1, w_ffn2):
    B, T, _ = x.shape
    A, R = ATT_WIDTH, RWKV_WIDTH
    cs = jax.nn.silu(c)
    v_first = None
    for l in range(DEPTH):
        mod = (cs @ w_ada[l] + b_ada[l])[:, None, :]
        sh1, sc1, gt1, sh2, sc2, gt2 = jnp.split(mod, 6, axis=-1)
        h = rmsnorm(x, g_pre_mix[l]) * (1.0 + sc1) + sh1
        proj = h @ w_in[l]
        qa, ka, va, rr, kr, vr = jnp.split(proj, [A, 2 * A, 3 * A, 3 * A + R, 3 * A + 2 * R], axis=-1)
        att = neighbourhood_attention(qa.reshape(B, T, ATT_HEADS, ATT_HEAD_DIM),
                                      ka.reshape(B, T, ATT_HEADS, ATT_HEAD_DIM),
                                      va.reshape(B, T, ATT_HEADS, ATT_HEAD_DIM), rpb[l])
        att = rmsnorm(att.reshape(B, T, A), g_att_out[l])
        vres = None if l == 0 else (mu_v[l - 1], v0[l - 1], v1[l - 1], v2[l - 1])
        rw, v_first = rwkv_time_mix(h, rr, kr, vr, v_first, mu_rkv[l], mu_x[l], w0[l], w1[l], w2[l],
                                    a0[l], a1[l], a2[l], g1[l], g2[l], k_k[l], k_a[l], r_k[l],
                                    ln_x_w[l], ln_x_b[l], vres)
        mix = jnp.concatenate([att, rw], axis=-1) @ w_out[l]
        x = x + gt1 * rmsnorm(mix, g_post_mix[l])
        h2 = rmsnorm(x, g_pre_ffn[l]) * (1.0 + sc2) + sh2
        f = jnp.square(jax.nn.relu(h2 @ w_ffn1[l])) @ w_ffn2[l]
        x = x + gt2 * rmsnorm(f, g_post_ffn[l])
    return x


def setup_inputs(seed: int = 0) -> dict:
    key = jax.random.key(seed)
    ks = jax.random.split(key, 40)
    f32 = jnp.float32
    L, D, A, R = DEPTH, D_MODEL, ATT_WIDTH, RWKV_WIDTH

    def nrm(k, shape, scale):
        return jax.random.normal(k, shape, f32) * scale

    def gain(k, shape):
        return 1.0 + 0.02 * jax.random.normal(k, shape, f32)

    def uni(k, shape, lo, hi):
        return jax.random.uniform(k, shape, f32, minval=lo, maxval=hi)

    return {
        "x_prompt": nrm(ks[0], (BATCH, SEQ, D), 1.0),
        "x_sample": nrm(ks[1], (DEC_BATCH, DEC_SEQ, D), 1.0),
        "c_prompt": nrm(ks[2], (BATCH, D), 1.0),
        "c_sample": nrm(ks[3], (DEC_BATCH, D), 1.0),
        "w_ada": nrm(ks[4], (L, D, 6 * D), 0.5 * D ** -0.5),
        "b_ada": nrm(ks[5], (L, 6 * D), 0.02),
        "g_pre_mix": gain(ks[6], (L, D)),
        "g_post_mix": gain(ks[7], (L, D)),
        "g_pre_ffn": gain(ks[8], (L, D)),
        "g_post_ffn": gain(ks[9], (L, D)),
        "w_in": nrm(ks[10], (L, D, IN_WIDTH), D ** -0.5),
        "rpb": nrm(ks[11], (L, ATT_HEADS, 2 * NA_ROWS - 1, 2 * NA_COLS - 1), 0.5),
        "g_att_out": gain(ks[12], (L, A)),
        "mu_rkv": uni(ks[13], (L, 3, 2, R), 0.0, 0.5),
        "mu_x": uni(ks[14], (L, 3, 2, D), 0.0, 0.5),
        "w0": uni(ks[15], (L, 2, R), -5.0, -1.0),
        "w1": nrm(ks[16], (L, 2, D, DECAY_LORA), D ** -0.5),
        "w2": nrm(ks[17], (L, 2, DECAY_LORA, R), 0.3 * DECAY_LORA ** -0.5),
        "a0": nrm(ks[18], (L, 2, R), 0.5),
        "a1": nrm(ks[19], (L, 2, D, AAA_LORA), D ** -0.5),
        "a2": nrm(ks[20], (L, 2, AAA_LORA, R), 0.3 * AAA_LORA ** -0.5),
        "g1": nrm(ks[21], (L, D, GATE_LORA), D ** -0.5),
        "g2": nrm(ks[22], (L, GATE_LORA, R), GATE_LORA ** -0.5),
        "mu_v": uni(ks[23], (L - 1, 2, D), 0.0, 0.5),
        "v0": nrm(ks[24], (L - 1, R), 0.5),
        "v1": nrm(ks[25], (L - 1, D, MV_LORA), D ** -0.5),
        "v2": nrm(ks[26], (L - 1, MV_LORA, R), 0.3 * MV_LORA ** -0.5),
        "k_k": 0.85 + 0.1 * jax.random.normal(ks[27], (L, R), f32),
        "k_a": 1.0 + 0.05 * jax.random.normal(ks[28], (L, R), f32),
        "r_k": nrm(ks[29], (L, RWKV_HEADS, RWKV_HEAD_DIM), 0.1),
        "ln_x_w": gain(ks[30], (L, R)),
        "ln_x_b": nrm(ks[31], (L, R), 0.02),
        "w_out": nrm(ks[32], (L, A + R, D), (A + R) ** -0.5),
        "w_ffn1": nrm(ks[33], (L, D, D_FF), D ** -0.5),
        "w_ffn2": nrm(ks[34], (L, D_FF, D), D_FF ** -0.5),
    }


def reference(x_prompt, x_sample, c_prompt, c_sample, w_ada, b_ada, g_pre_mix, g_post_mix,
              g_pre_ffn, g_post_ffn, w_in, rpb, g_att_out, mu_rkv, mu_x, w0, w1, w2, a0, a1, a2,
              g1, g2, mu_v, v0, v1, v2, k_k, k_a, r_k, ln_x_w, ln_x_b, w_out, w_ffn1, w_ffn2):
    y_prompt = trunk(x_prompt, c_prompt, w_ada, b_ada, g_pre_mix, g_post_mix, g_pre_ffn, g_post_ffn,
                     w_in, rpb, g_att_out, mu_rkv, mu_x, w0, w1, w2, a0, a1, a2, g1, g2, mu_v, v0, v1, v2,
                     k_k, k_a, r_k, ln_x_w, ln_x_b, w_out, w_ffn1, w_ffn2)
    y_sample = trunk(x_sample, c_sample, w_ada, b_ada, g_pre_mix, g_post_mix, g_pre_ffn, g_post_ffn,
                     w_in, rpb, g_att_out, mu_rkv, mu_x, w0, w1, w2, a0, a1, a2, g1, g2, mu_v, v0, v1, v2,
                     k_k, k_a, r_k, ln_x_w, ln_x_b, w_out, w_ffn1, w_ffn2)
    return (y_prompt, y_sample)
```

```python
import functools
import math

import numpy as np
import jax
import jax.numpy as jnp
from jax import lax
from jax.experimental import pallas as pl
from jax.experimental.pallas import tpu as pltpu

F32 = jnp.float32
BF16 = jnp.bfloat16
HIGHEST = lax.Precision.HIGHEST

D_MODEL = 2048
DEPTH = 4
GRID_W = 64
ATT_WIDTH = 1024
RWKV_WIDTH = 1024
ATT_HEAD_DIM = 128
ATT_HEADS = 8
RWKV_HEAD_DIM = 64
NA_ROWS = 8
NA_COLS = 16
D_FF = 4 * D_MODEL
IN_WIDTH = 3 * ATT_WIDTH + 3 * RWKV_WIDTH
NORM_EPS = 1e-6
GN_EPS = 64e-5
LORA_TILE = 128
LORA_COLS = 3 * LORA_TILE
LORA_ALL = 3 * LORA_COLS
CHUNK = 64
GROUP_LANES = 128
N_GROUPS = RWKV_WIDTH // GROUP_LANES
MASK_NEG = -1e30
VMEM_LIMIT = 56 * 1024 * 1024


def _sigmoid(x):
    return 1.0 / (1.0 + jnp.exp(-x))


def _dot(a, b, precision=None):
    return jnp.dot(a, b, preferred_element_type=F32, precision=precision)


def _dot_nt(a, b, precision=None):
    return lax.dot_general(a, b, (((1,), (1,)), ((), ())), preferred_element_type=F32, precision=precision)


def _dot_tn(a, b, precision=None):
    return lax.dot_general(a, b, (((0,), (0,)), ((), ())), preferred_element_type=F32, precision=precision)


def _rms(x, g):
    return x * lax.rsqrt(jnp.mean(x * x, axis=-1, keepdims=True) + NORM_EPS) * g


def _ada_kernel(c_ref, w_ref, b_ref, o_ref):
    c = c_ref[...]
    cs = c * _sigmoid(c)
    o_ref[0] = _dot(cs, w_ref[0], HIGHEST) + b_ref[0]


def _ada_call(c_all, w_ada, b_ada):
    L, _, n6 = w_ada.shape
    nb = c_all.shape[0]
    tn = 1024
    return pl.pallas_call(
        _ada_kernel,
        grid=(L, n6 // tn),
        in_specs=[
            pl.BlockSpec((nb, D_MODEL), lambda l, j: (0, 0)),
            pl.BlockSpec((1, D_MODEL, tn), lambda l, j: (l, 0, j)),
            pl.BlockSpec((1, 1, tn), lambda l, j: (l, 0, j)),
        ],
        out_specs=pl.BlockSpec((1, nb, tn), lambda l, j: (l, 0, j)),
        out_shape=jax.ShapeDtypeStruct((L, nb, n6), F32),
        compiler_params=pltpu.CompilerParams(
            dimension_semantics=("arbitrary", "arbitrary"), vmem_limit_bytes=VMEM_LIMIT),
        name="ada_mod",
    )(c_all, w_ada, b_ada.reshape(L, 1, n6))


def _inproj_kernel(tb_ref, x_ref, mod_ref, g_ref, w_ref, wl_ref, proj_ref, p_ref, h_scr):
    del tb_ref

    @pl.when(pl.program_id(1) == 0)
    def _():
        h = _rms(x_ref[...], g_ref[...]) * (1.0 + mod_ref[0, 1:2, :]) + mod_ref[0, 0:1, :]
        hb = h.astype(BF16)
        h_scr[...] = hb
        p_ref[...] = _dot(hb, wl_ref[...])

    proj_ref[...] = _dot(h_scr[...], w_ref[...])


def _inproj_call(x, mod_l, g_pre, w_in_bf, wl_bf, tile_batch, tm):
    n_tok = x.shape[0]
    tn = 1024
    grid_spec = pltpu.PrefetchScalarGridSpec(
        num_scalar_prefetch=1,
        grid=(n_tok // tm, IN_WIDTH // tn),
        in_specs=[
            pl.BlockSpec((tm, D_MODEL), lambda i, n, tb: (i, 0)),
            pl.BlockSpec((1, 6, D_MODEL), lambda i, n, tb: (tb[i], 0, 0)),
            pl.BlockSpec((1, D_MODEL), lambda i, n, tb: (0, 0)),
            pl.BlockSpec((D_MODEL, tn), lambda i, n, tb: (0, n)),
            pl.BlockSpec((D_MODEL, LORA_ALL), lambda i, n, tb: (0, 0)),
        ],
        out_specs=[
            pl.BlockSpec((tm, tn), lambda i, n, tb: (i, n)),
            pl.BlockSpec((tm, LORA_ALL), lambda i, n, tb: (i, 0)),
        ],
        scratch_shapes=[pltpu.VMEM((tm, D_MODEL), BF16)],
    )
    return pl.pallas_call(
        _inproj_kernel,
        grid_spec=grid_spec,
        out_shape=[jax.ShapeDtypeStruct((n_tok, IN_WIDTH), F32),
                   jax.ShapeDtypeStruct((n_tok, LORA_ALL), F32)],
        compiler_params=pltpu.CompilerParams(
            dimension_semantics=("arbitrary", "arbitrary"), vmem_limit_bytes=VMEM_LIMIT),
        name="in_proj",
    )(tile_batch, x, mod_l, g_pre, w_in_bf, wl_bf)


def _att_kernel(q_ref, k_ref, v_ref, bias_ref, o_ref, kb_ref, vb_ref, *, rows, qrows):
    i = pl.program_id(2)

    @pl.when(i == 0)
    def _():
        kb_ref[...] = k_ref[...].astype(BF16)
        vb_ref[...] = v_ref[...].astype(BF16)

    scale = ATT_HEAD_DIM ** -0.5
    win = NA_ROWS * GRID_W

    def body(rr, carry):
        r = i * qrows + rr
        rs = jnp.clip(r - NA_ROWS // 2, 0, rows - NA_ROWS)
        off = r - rs
        q = q_ref[pl.ds(pl.multiple_of(rr * GRID_W, GRID_W), GRID_W), :].astype(BF16)
        start = pl.multiple_of(rs * GRID_W, GRID_W)
        kw = kb_ref[pl.ds(start, win), :]
        vw = vb_ref[pl.ds(start, win), :]
        s = _dot_nt(q, kw) * scale + bias_ref[0, off]
        m = jnp.max(s, axis=-1, keepdims=True)
        p = jnp.exp(s - m)
        l = jnp.sum(p, axis=-1, keepdims=True)
        pv = _dot(p.astype(BF16), vw)
        o_ref[pl.ds(pl.multiple_of(rr * GRID_W, GRID_W), GRID_W), :] = pv / l
        return carry

    lax.fori_loop(0, qrows, body, 0)


def _att_call(proj, bias_tab, tok_base, n_seq, seq_len, att_out_shape):
    rows = seq_len // GRID_W
    assert rows >= NA_ROWS and seq_len % GRID_W == 0
    qrows = 8
    qtok = qrows * GRID_W
    assert tok_base % seq_len == 0 and seq_len % qtok == 0
    seq0 = tok_base // seq_len
    qblk0 = tok_base // qtok
    nq = seq_len // qtok
    n_tok = n_seq * seq_len
    kern = functools.partial(_att_kernel, rows=rows, qrows=qrows)
    return pl.pallas_call(
        kern,
        grid=(n_seq, ATT_HEADS, nq),
        in_specs=[
            pl.BlockSpec((qtok, ATT_HEAD_DIM), lambda b, h, i: (qblk0 + b * nq + i, h)),
            pl.BlockSpec((seq_len, ATT_HEAD_DIM), lambda b, h, i: (seq0 + b, ATT_HEADS + h)),
            pl.BlockSpec((seq_len, ATT_HEAD_DIM), lambda b, h, i: (seq0 + b, 2 * ATT_HEADS + h)),
            pl.BlockSpec((1, NA_ROWS, GRID_W, NA_ROWS * GRID_W), lambda b, h, i: (h, 0, 0, 0)),
        ],
        out_specs=pl.BlockSpec((qtok, ATT_HEAD_DIM), lambda b, h, i: (b * nq + i, h)),
        out_shape=jax.ShapeDtypeStruct((n_tok, ATT_WIDTH), F32),
        scratch_shapes=[pltpu.VMEM((seq_len, ATT_HEAD_DIM), BF16),
                        pltpu.VMEM((seq_len, ATT_HEAD_DIM), BF16)],
        compiler_params=pltpu.CompilerParams(
            dimension_semantics=("arbitrary", "arbitrary", "arbitrary"), vmem_limit_bytes=VMEM_LIMIT),
        name="nbr_attention",
    )(proj, proj, proj, bias_tab)


def _att_bias_table(rpb_l):
    o = np.arange(NA_ROWS)[:, None, None, None]
    qc = np.arange(GRID_W)[None, :, None, None]
    j = np.arange(NA_ROWS)[None, None, :, None]
    kc = np.arange(GRID_W)[None, None, None, :]
    cs = np.clip(qc - NA_COLS // 2, 0, GRID_W - NA_COLS)
    valid = (kc >= cs) & (kc < cs + NA_COLS)
    dr = np.broadcast_to(j - o + NA_ROWS - 1, (NA_ROWS, GRID_W, NA_ROWS, GRID_W))
    dc = np.broadcast_to(np.clip(kc - qc + NA_COLS - 1, 0, 2 * NA_COLS - 2), dr.shape)
    valid = np.broadcast_to(valid, dr.shape)
    tab = rpb_l[:, dr, dc]
    tab = jnp.where(valid[None], tab, MASK_NEG)
    return tab.reshape(ATT_HEADS, NA_ROWS, GRID_W, NA_ROWS * GRID_W).astype(F32)


_PV_MU_R, _PV_MU_K, _PV_MU_V, _PV_W0, _PV_A0, _PV_V0, _PV_KK, _PV_KA = 0, 2, 4, 6, 8, 10, 11, 12
_PV_ROWS = 16


def _shift_prev(u, halo_row, row_id):
    return jnp.where(row_id == 0, halo_row, pltpu.roll(u, 1, axis=0))


def _shift_next(u, halo_row, row_id):
    n = u.shape[0]
    return jnp.where(row_id == n - 1, halo_row, pltpu.roll(u, n - 1, axis=0))


def _prep_kernel(first_ref, last_ref, r_ref, k_ref, v_ref, hp_ref, hn_ref, p_ref, php_ref, phn_ref,
                 pv_ref, w2_ref, *rest, has_vres):
    if has_vres:
        vf_ref, outs = rest[0], rest[1:]
    else:
        vf_ref, outs = None, rest
    i = pl.program_id(0)
    tm = r_ref.shape[0]
    keep_p = 1.0 - first_ref[i].astype(F32)
    keep_n = 1.0 - last_ref[i].astype(F32)
    row_id = lax.broadcasted_iota(jnp.int32, (tm, 1), 0)
    hp = hp_ref[7:8, :] * keep_p
    hn = hn_ref[0:1, :] * keep_n
    W = RWKV_WIDTH

    def pvec(row):
        return pv_ref[row:row + 1, :]

    def tshift(u, halo_p, halo_n, mu_row):
        prev = _shift_prev(u, halo_p, row_id)
        nxt = _shift_next(u, halo_n, row_id)
        return u + pvec(mu_row) * (prev - u) + pvec(mu_row + 1) * (nxt - u)

    r = tshift(r_ref[...], hp[:, 0:W], hn[:, 0:W], _PV_MU_R)
    k = tshift(k_ref[...], hp[:, W:2 * W], hn[:, W:2 * W], _PV_MU_K)
    v = tshift(v_ref[...], hp[:, 2 * W:3 * W], hn[:, 2 * W:3 * W], _PV_MU_V)

    C = LORA_COLS
    pall = p_ref[...]
    pre = (pall[:, 0:C]
           + _shift_prev(pall[:, C:2 * C], php_ref[7:8, C:2 * C] * keep_p, row_id)
           + _shift_next(pall[:, 2 * C:3 * C], phn_ref[0:1, 2 * C:3 * C] * keep_n, row_id))
    T = LORA_TILE
    tw = jnp.tanh(pre[:, 0:T])
    pa = pre[:, T:2 * T]
    pgv = pre[:, 2 * T:3 * T]

    if has_vres:
        vgate = _sigmoid(pvec(_PV_V0) + _dot(pgv, w2_ref[5], HIGHEST))
        v = v + (vf_ref[...] - v) * vgate
    gate = _dot(_sigmoid(pgv), w2_ref[4], HIGHEST)
    kku = k * pvec(_PV_KK)

    ri = lax.broadcasted_iota(jnp.int32, (tm, tm), 0)
    ci = lax.broadcasted_iota(jnp.int32, (tm, tm), 1)
    same = (ri // CHUNK) == (ci // CHUNK)
    tri = (jnp.where(same & (ci <= ri), 1.0, 0.0).astype(F32),
           jnp.where(same & (ci >= ri), 1.0, 0.0).astype(F32))

    per_dir = []
    for e in range(2):
        wlog = pvec(_PV_W0 + e) + _dot(tw, w2_ref[e], HIGHEST)
        lw = -math.exp(-0.5) * _sigmoid(wlog)
        a = _sigmoid(pvec(_PV_A0 + e) + _dot(pa, w2_ref[2 + e], HIGHEST))
        kd = k * (1.0 + (a - 1.0) * pvec(_PV_KA))
        cum = _dot(tri[e], lw, HIGHEST)
        per_dir.append((cum, kd, a))

    arrays = [r, v, kku, gate,
              per_dir[0][0], per_dir[0][1], per_dir[0][2],
              per_dir[1][0], per_dir[1][1], per_dir[1][2]]
    for arr, o_ref in zip(arrays, outs[:10]):
        for g in range(N_GROUPS):
            o_ref[g] = arr[:, g * GROUP_LANES:(g + 1) * GROUP_LANES]
    if not has_vres:
        outs[10][...] = v


def _prep_call(proj, p_lora, pvec, w2, v_first, first_flags, last_flags, tm):
    n_tok = proj.shape[0]
    has_vres = v_first is not None
    hb = tm // 8
    nb8 = n_tok // 8
    W = RWKV_WIDTH

    def cur(col):
        return lambda i, f, l: (i, col)

    def prev8(col):
        return lambda i, f, l: (jnp.maximum(i * hb - 1, 0), col)

    def next8(col):
        return lambda i, f, l: (jnp.minimum((i + 1) * hb, nb8 - 1), col)

    in_specs = [
        pl.BlockSpec((tm, W), cur(3)), pl.BlockSpec((tm, W), cur(4)), pl.BlockSpec((tm, W), cur(5)),
        pl.BlockSpec((8, 3 * W), prev8(1)), pl.BlockSpec((8, 3 * W), next8(1)),
        pl.BlockSpec((tm, LORA_ALL), cur(0)),
        pl.BlockSpec((8, LORA_ALL), prev8(0)), pl.BlockSpec((8, LORA_ALL), next8(0)),
        pl.BlockSpec((_PV_ROWS, W), lambda i, f, l: (0, 0)),
        pl.BlockSpec((6, LORA_TILE, W), lambda i, f, l: (0, 0, 0)),
    ]
    args = [proj, proj, proj, proj, proj, p_lora, p_lora, p_lora, pvec, w2]
    if has_vres:
        in_specs.append(pl.BlockSpec((tm, W), cur(0)))
        args.append(v_first)
    gshape = jax.ShapeDtypeStruct((N_GROUPS, n_tok, GROUP_LANES), F32)
    gspec = pl.BlockSpec((N_GROUPS, tm, GROUP_LANES), lambda i, f, l: (0, i, 0))
    out_shape = [gshape] * 10
    out_specs = [gspec] * 10
    if not has_vres:
        out_shape = out_shape + [jax.ShapeDtypeStruct((n_tok, W), F32)]
        out_specs = out_specs + [pl.BlockSpec((tm, W), cur(0))]
    grid_spec = pltpu.PrefetchScalarGridSpec(
        num_scalar_prefetch=2, grid=(n_tok // tm,), in_specs=in_specs, out_specs=out_specs)
    return pl.pallas_call(
        functools.partial(_prep_kernel, has_vres=has_vres),
        grid_spec=grid_spec,
        out_shape=out_shape,
        compiler_params=pltpu.CompilerParams(
            dimension_semantics=("arbitrary",), vmem_limit_bytes=VMEM_LIMIT),
        name="rwkv_prep",
    )(first_flags, last_flags, *args)


def _head_sum(x, m0):
    s0 = jnp.sum(jnp.where(m0, x, 0.0), axis=-1, keepdims=True)
    s1 = jnp.sum(jnp.where(m0, 0.0, x), axis=-1, keepdims=True)
    return jnp.where(m0, s0, s1)


def _stack_heads(z, m0):
    return jnp.concatenate([jnp.where(m0, z, 0.0), jnp.where(m0, 0.0, z)], axis=0)


def _wkv_kernel(blk_ref, first_ref, r_ref, v_ref, kku_ref, cum_ref, kd_ref, ad_ref, rk_ref, *rest,
                reverse, final):
    del blk_ref
    if final:
        yprev_ref, gate_ref, lnw_ref, lnb_ref, o_ref, state_ref = rest
    else:
        o_ref, state_ref = rest
    step = pl.program_id(0)
    C = CHUNK
    G = GROUP_LANES
    hd = RWKV_HEAD_DIM

    @pl.when(first_ref[step] == 1)
    def _():
        state_ref[...] = jnp.zeros_like(state_ref)

    lane = lax.broadcasted_iota(jnp.int32, (1, G), 1)
    m0 = lane < hd
    row = lax.broadcasted_iota(jnp.int32, (C, 1), 0)
    t_id = lax.broadcasted_iota(jnp.int32, (C, G), 0)
    s_id = lax.broadcasted_iota(jnp.int32, (C, G), 1) % hd
    if reverse:
        strict, incl = s_id > t_id, s_id >= t_id
    else:
        strict, incl = s_id < t_id, s_id <= t_id
    r2 = lax.broadcasted_iota(jnp.int32, (G, G), 0)
    c2 = lax.broadcasted_iota(jnp.int32, (G, G), 1)
    eye = (r2 == c2).astype(F32)
    same_head = (r2 // hd) == (c2 // hd)
    hp = HIGHEST

    for g in range(N_GROUPS):
        r = r_ref[g]
        v = v_ref[g]
        kku = kku_ref[g]
        cum = cum_ref[g]
        kd = kd_ref[g]
        ad = ad_ref[g]
        kk = kku * lax.rsqrt(jnp.maximum(_head_sum(kku * kku, m0), 1e-24))
        a = -kk
        b = kk * ad
        if reverse:
            cum_x = jnp.where(row == C - 1, 0.0, pltpu.roll(cum, C - 1, axis=0))
            ltot = cum[0:1, :]
        else:
            cum_x = jnp.where(row == 0, 0.0, pltpu.roll(cum, 1, axis=0))
            ltot = cum[C - 1:C, :]
        e_neg = jnp.exp(-cum)
        e_tot = jnp.exp(ltot)
        at = a * jnp.exp(cum_x)
        rt = r * jnp.exp(cum)
        bt = b * e_neg
        kt = kd * e_neg
        bh = bt * e_tot
        kh = kt * e_tot

        lhs = jnp.concatenate([at, rt], axis=0)
        rhs = jnp.concatenate([_stack_heads(bt, m0), _stack_heads(kt, m0)], axis=0)
        q = _dot_nt(lhs, rhs, hp)
        a_ab = jnp.where(strict, q[:C, :G], 0.0)
        a_ak = jnp.where(strict, q[:C, G:], 0.0)
        a_rb = jnp.where(incl, q[C:, :G], 0.0)
        a_rk = jnp.where(incl, q[C:, G:], 0.0)

        x = _stack_heads(a_ab, m0)
        t_inv = eye + x
        pw = x
        for _ in range(5):
            pw = _dot(pw, pw, hp)
            t_inv = t_inv + _dot(t_inv, pw, hp)
        t_cat = t_inv[:C] + t_inv[C:]

        vs = _stack_heads(v, m0)
        akv = _dot(a_ak, vs, hp)
        tz = _dot(t_cat, jnp.concatenate([_stack_heads(at, m0), _stack_heads(akv, m0)], axis=1), hp)
        atp = tz[:, :G]
        vp = tz[:, G:]
        w_cat = jnp.concatenate([a_rb, a_rk], axis=1)
        big = jnp.concatenate([
            jnp.concatenate([_stack_heads(atp, m0), _stack_heads(vp, m0)], axis=1),
            jnp.concatenate([jnp.zeros((G, G), F32), vs], axis=1)], axis=0)
        ry = _dot(w_cat, big, hp)
        rp = rt + ry[:, :G]
        yi = ry[:, G:]
        mt = jnp.where(same_head, _dot_tn(bh, atp, hp), 0.0) + eye * e_tot
        gt = jnp.where(same_head,
                       _dot_tn(jnp.concatenate([bh, kh], axis=0), jnp.concatenate([vp, v], axis=0), hp), 0.0)

        p_state = state_ref[g]
        y = _dot(rp, p_state, hp) + yi
        state_ref[g] = _dot(mt, p_state, hp) + gt
        y = y + _head_sum(r * kd * rk_ref[g], m0) * v
        if final:
            y = y + yprev_ref[g]
            mean = _head_sum(y, m0) * (1.0 / hd)
            dlt = y - mean
            var = _head_sum(dlt * dlt, m0) * (1.0 / hd)
            yn = dlt * lax.rsqrt(var + GN_EPS)
            o_ref[g] = (yn * lnw_ref[g] + lnb_ref[g]) * gate_ref[g]
        else:
            o_ref[g] = y


def _wkv_call(r, v, kku, cum, kd, ad, rk_g, blk, first, reverse, final_args=None):
    n_tok = r.shape[1]
    n_chunks = n_tok // CHUNK
    final = final_args is not None
    cspec = pl.BlockSpec((N_GROUPS, CHUNK, GROUP_LANES), lambda s, b, f: (0, b[s], 0))
    pspec = pl.BlockSpec((N_GROUPS, 1, GROUP_LANES), lambda s, b, f: (0, 0, 0))
    in_specs = [cspec] * 6 + [pspec]
    args = [r, v, kku, cum, kd, ad, rk_g]
    if final:
        y_prev, gate, lnw_g, lnb_g = final_args
        in_specs += [cspec, cspec, pspec, pspec]
        args += [y_prev, gate, lnw_g, lnb_g]
    grid_spec = pltpu.PrefetchScalarGridSpec(
        num_scalar_prefetch=2, grid=(n_chunks,), in_specs=in_specs, out_specs=cspec,
        scratch_shapes=[pltpu.VMEM((N_GROUPS, GROUP_LANES, GROUP_LANES), F32)])
    return pl.pallas_call(
        functools.partial(_wkv_kernel, reverse=reverse, final=final),
        grid_spec=grid_spec,
        out_shape=jax.ShapeDtypeStruct((N_GROUPS, n_tok, GROUP_LANES), F32),
        compiler_params=pltpu.CompilerParams(
            dimension_semantics=("arbitrary",), vmem_limit_bytes=VMEM_LIMIT),
        name="wkv_bwd" if reverse else "wkv_fwd",
    )(blk, first, *args)


def _outproj_kernel(tb_ref, x_ref, att_ref, rw_ref, mod_ref, gatt_ref, gpost_ref, w_ref, o_ref):
    del tb_ref
    att_n = _rms(att_ref[...], gatt_ref[...]).astype(BF16)
    rw = jnp.concatenate([rw_ref[g] for g in range(N_GROUPS)], axis=1).astype(BF16)
    mix = _dot(att_n, w_ref[0:ATT_WIDTH, :]) + _dot(rw, w_ref[ATT_WIDTH:, :])
    o_ref[...] = x_ref[...] + mod_ref[0, 2:3, :] * _rms(mix, gpost_ref[...])


def _outproj_call(x, att, rw_g, mod_l, g_att, g_post, w_out_bf, tile_batch, tm):
    n_tok = x.shape[0]
    grid_spec = pltpu.PrefetchScalarGridSpec(
        num_scalar_prefetch=1,
        grid=(n_tok // tm,),
        in_specs=[
            pl.BlockSpec((tm, D_MODEL), lambda i, tb: (i, 0)),
            pl.BlockSpec((tm, ATT_WIDTH), lambda i, tb: (i, 0)),
            pl.BlockSpec((N_GROUPS, tm, GROUP_LANES), lambda i, tb: (0, i, 0)),
            pl.BlockSpec((1, 6, D_MODEL), lambda i, tb: (tb[i], 0, 0)),
            pl.BlockSpec((1, ATT_WIDTH), lambda i, tb: (0, 0)),
            pl.BlockSpec((1, D_MODEL), lambda i, tb: (0, 0)),
            pl.BlockSpec((ATT_WIDTH + RWKV_WIDTH, D_MODEL), lambda i, tb: (0, 0)),
        ],
        out_specs=pl.BlockSpec((tm, D_MODEL), lambda i, tb: (i, 0)),
    )
    return pl.pallas_call(
        _outproj_kernel,
        grid_spec=grid_spec,
        out_shape=jax.ShapeDtypeStruct((n_tok, D_MODEL), F32),
        compiler_params=pltpu.CompilerParams(
            dimension_semantics=("arbitrary",), vmem_limit_bytes=VMEM_LIMIT),
        name="out_proj",
    )(tile_batch, x, att, rw_g, mod_l, g_att, g_post, w_out_bf)


def _ffn_kernel(tb_ref, x_ref, mod_ref, gpre_ref, gpost_ref, w1_ref, w2_ref, o_ref, h_scr, acc_scr):
    del tb_ref
    f = pl.program_id(1)

    @pl.when(f == 0)
    def _():
        h = _rms(x_ref[...], gpre_ref[...]) * (1.0 + mod_ref[0, 4:5, :]) + mod_ref[0, 3:4, :]
        h_scr[...] = h.astype(BF16)
        acc_scr[...] = jnp.zeros_like(acc_scr)

    u = jnp.maximum(_dot(h_scr[...], w1_ref[...]), 0.0)
    acc_scr[...] += _dot((u * u).astype(BF16), w2_ref[...])

    @pl.when(f == pl.num_programs(1) - 1)
    def _():
        o_ref[...] = x_ref[...] + mod_ref[0, 5:6, :] * _rms(acc_scr[...], gpost_ref[...])


def _ffn_call(x, mod_l, g_pre, g_post, w1_bf, w2_bf, tile_batch, tm):
    n_tok = x.shape[0]
    tf = 512
    grid_spec = pltpu.PrefetchScalarGridSpec(
        num_scalar_prefetch=1,
        grid=(n_tok // tm, D_FF // tf),
        in_specs=[
            pl.BlockSpec((tm, D_MODEL), lambda i, f, tb: (i, 0)),
            pl.BlockSpec((1, 6, D_MODEL), lambda i, f, tb: (tb[i], 0, 0)),
            pl.BlockSpec((1, D_MODEL), lambda i, f, tb: (0, 0)),
            pl.BlockSpec((1, D_MODEL), lambda i, f, tb: (0, 0)),
            pl.BlockSpec((D_MODEL, tf), lambda i, f, tb: (0, f)),
            pl.BlockSpec((tf, D_MODEL), lambda i, f, tb: (f, 0)),
        ],
        out_specs=pl.BlockSpec((tm, D_MODEL), lambda i, f, tb: (i, 0)),
        scratch_shapes=[pltpu.VMEM((tm, D_MODEL), BF16), pltpu.VMEM((tm, D_MODEL), F32)],
    )
    return pl.pallas_call(
        _ffn_kernel,
        grid_spec=grid_spec,
        out_shape=jax.ShapeDtypeStruct((n_tok, D_MODEL), F32),
        compiler_params=pltpu.CompilerParams(
            dimension_semantics=("arbitrary", "arbitrary"), vmem_limit_bytes=VMEM_LIMIT),
        name="ffn",
    )(tile_batch, x, mod_l, g_pre, g_post, w1_bf, w2_bf)


def _lora_in_weights(l, mu_x, w1, a1, g1, mu_v, v1):
    zeros32 = jnp.zeros((D_MODEL, 32), F32)
    if l == 0:
        mv = jnp.zeros((2, D_MODEL), F32)
        v1l = zeros32
    else:
        mv = mu_v[l - 1]
        v1l = v1[l - 1]
    mats = [(w1[l, 0], mu_x[l, 0]), (w1[l, 1], mu_x[l, 0]), (a1[l, 0], mu_x[l, 1]), (a1[l, 1], mu_x[l, 1]),
            (g1[l], mu_x[l, 2]), (v1l, mv), (zeros32, jnp.zeros((2, D_MODEL), F32))]
    groups = []
    for which in range(3):
        cols = []
        for w, mu in mats:
            if which == 0:
                s = 1.0 - mu[0] - mu[1]
            elif which == 1:
                s = mu[0]
            else:
                s = mu[1]
            cols.append(w * s[:, None])
        groups.append(jnp.concatenate(cols, axis=1))
    return jnp.concatenate(groups, axis=1).astype(BF16)


def _lora_out_weights(l, w2, a2, g2, v2):
    z64 = jnp.zeros((64, RWKV_WIDTH), F32)
    z32 = jnp.zeros((32, RWKV_WIDTH), F32)
    v2l = z32 if l == 0 else v2[l - 1]
    return jnp.stack([
        jnp.concatenate([w2[l, 0], z64], axis=0),
        jnp.concatenate([z64, w2[l, 1]], axis=0),
        jnp.concatenate([a2[l, 0], z64], axis=0),
        jnp.concatenate([z64, a2[l, 1]], axis=0),
        jnp.concatenate([g2[l], z64], axis=0),
        jnp.concatenate([z64, v2l, z32], axis=0),
    ])


def _channel_pack(l, mu_rkv, w0, a0, v0, k_k, k_a):
    z = jnp.zeros((1, RWKV_WIDTH), F32)
    v0l = z if l == 0 else v0[l - 1][None]
    rows = [mu_rkv[l].reshape(6, RWKV_WIDTH), w0[l], a0[l], v0l, k_k[l][None], k_a[l][None],
            jnp.zeros((_PV_ROWS - 13, RWKV_WIDTH), F32)]
    return jnp.concatenate(rows, axis=0)


def _to_groups(vec):
    return vec.reshape(N_GROUPS, 1, GROUP_LANES)


def _layout(trunks, tile):
    tb, first, last = [], [], []
    b0 = 0
    for n_seq, seq_len in trunks:
        per = seq_len // tile
        for b in range(n_seq):
            for t in range(per):
                tb.append(b0 + b)
                first.append(1 if t == 0 else 0)
                last.append(1 if t == per - 1 else 0)
        b0 += n_seq
    return (jnp.asarray(tb, jnp.int32), jnp.asarray(first, jnp.int32), jnp.asarray(last, jnp.int32))


def _forward(xs, cs, w_ada, b_ada, g_pre_mix, g_post_mix, g_pre_ffn, g_post_ffn, w_in, rpb, g_att_out,
             mu_rkv, mu_x, w0, w1, w2, a0, a1, a2, g1, g2, mu_v, v0, v1, v2, k_k, k_a, r_k,
             ln_x_w, ln_x_b, w_out, w_ffn1, w_ffn2, tm_big=512, tm_prep=256):
    trunks = [(x.shape[0], x.shape[1]) for x in xs]
    x = jnp.concatenate([xx.reshape(-1, D_MODEL) for xx in xs], axis=0)
    n_batch = sum(t[0] for t in trunks)
    nb_pad = -(-n_batch // 8) * 8
    c_all = jnp.concatenate(list(cs) + [jnp.zeros((nb_pad - n_batch, D_MODEL), F32)], axis=0)
    n_layers = w_in.shape[0]

    tb_big, _, _ = _layout(trunks, tm_big)
    _, first_prep, last_prep = _layout(trunks, tm_prep)
    _, first_chunk, last_chunk = _layout(trunks, CHUNK)
    n_chunks = first_chunk.shape[0]
    blk_fwd = jnp.arange(n_chunks, dtype=jnp.int32)
    blk_bwd = blk_fwd[::-1]
    first_bwd = last_chunk[::-1]

    mod = _ada_call(c_all, w_ada, b_ada).reshape(n_layers, nb_pad, 6, D_MODEL)

    v_first = None
    for l in range(n_layers):
        mod_l = mod[l]
        wl = _lora_in_weights(l, mu_x, w1, a1, g1, mu_v, v1)
        proj, p_lora = _inproj_call(x, mod_l, g_pre_mix[l][None], w_in[l].astype(BF16), wl, tb_big, tm_big)

        bias_tab = _att_bias_table(rpb[l])
        atts = []
        base = 0
        for n_seq, seq_len in trunks:
            atts.append(_att_call(proj, bias_tab, base, n_seq, seq_len, None))
            base += n_seq * seq_len
        att = jnp.concatenate(atts, axis=0)

        pvec = _channel_pack(l, mu_rkv, w0, a0, v0, k_k, k_a)
        w2pack = _lora_out_weights(l, w2, a2, g2, v2)
        outs = _prep_call(proj, p_lora, pvec, w2pack, v_first, first_prep, last_prep, tm_prep)
        r_g, v_g, kku_g, gate_g, cum0, kd0, ad0, cum1, kd1, ad1 = outs[:10]
        if l == 0:
            v_first = outs[10]
        rk_g = _to_groups(r_k[l].reshape(-1))
        y_f = _wkv_call(r_g, v_g, kku_g, cum0, kd0, ad0, rk_g, blk_fwd, first_chunk, reverse=False)
        rw_g = _wkv_call(r_g, v_g, kku_g, cum1, kd1, ad1, rk_g, blk_bwd, first_bwd, reverse=True,
                         final_args=(y_f, gate_g, _to_groups(ln_x_w[l]), _to_groups(ln_x_b[l])))

        x = _outproj_call(x, att, rw_g, mod_l, g_att_out[l][None], g_post_mix[l][None],
                          w_out[l].astype(BF16), tb_big, tm_big)
        x = _ffn_call(x, mod_l, g_pre_ffn[l][None], g_post_ffn[l][None],
                      w_ffn1[l].astype(BF16), w_ffn2[l].astype(BF16), tb_big, tm_big)

    outs = []
    base = 0
    for (n_seq, seq_len), xx in zip(trunks, xs):
        outs.append(x[base:base + n_seq * seq_len].reshape(xx.shape))
        base += n_seq * seq_len
    return tuple(outs)


def kernel(x_prompt, x_sample, c_prompt, c_sample, w_ada, b_ada, g_pre_mix, g_post_mix, g_pre_ffn, g_post_ffn, w_in, rpb, g_att_out, mu_rkv, mu_x, w0, w1, w2, a0, a1, a2, g1, g2, mu_v, v0, v1, v2, k_k, k_a, r_k, ln_x_w, ln_x_b, w_out, w_ffn1, w_ffn2):
    return _forward((x_prompt, x_sample), (c_prompt, c_sample), w_ada, b_ada, g_pre_mix, g_post_mix,
                    g_pre_ffn, g_post_ffn, w_in, rpb, g_att_out, mu_rkv, mu_x, w0, w1, w2, a0, a1, a2,
                    g1, g2, mu_v, v0, v1, v2, k_k, k_a, r_k, ln_x_w, ln_x_b, w_out, w_ffn1, w_ffn2)
```

```python
import functools
import math

import numpy as np
import jax
import jax.numpy as jnp
from jax import lax
from jax.experimental import pallas as pl
from jax.experimental.pallas import tpu as pltpu

F32 = jnp.float32
BF16 = jnp.bfloat16
HIGHEST = lax.Precision.HIGHEST

D_MODEL = 2048
DEPTH = 4
GRID_W = 64
ATT_WIDTH = 1024
RWKV_WIDTH = 1024
ATT_HEAD_DIM = 128
ATT_HEADS = 8
RWKV_HEAD_DIM = 64
NA_ROWS = 8
NA_COLS = 16
D_FF = 4 * D_MODEL
IN_WIDTH = 3 * ATT_WIDTH + 3 * RWKV_WIDTH
NORM_EPS = 1e-6
GN_EPS = 64e-5
LORA_TILE = 128
LORA_COLS = 3 * LORA_TILE
LORA_ALL = 3 * LORA_COLS
CHUNK = 64
GROUP_LANES = 128
N_GROUPS = RWKV_WIDTH // GROUP_LANES
MASK_NEG = -1e30
VMEM_LIMIT = 56 * 1024 * 1024


def _sigmoid(x):
    return 1.0 / (1.0 + jnp.exp(-x))


def _contract(a, b, dims, precision):
    dn = (dims, ((), ()))
    if precision == "bf16":
        return lax.dot_general(a.astype(BF16), b.astype(BF16), dn, preferred_element_type=F32)
    if precision == "bf16x3":
        a_hi = a.astype(BF16)
        a_lo = (a - a_hi.astype(F32)).astype(BF16)
        b_hi = b.astype(BF16)
        b_lo = (b - b_hi.astype(F32)).astype(BF16)

        def d(u, w):
            return lax.dot_general(u, w, dn, preferred_element_type=F32)

        return d(a_hi, b_hi) + (d(a_hi, b_lo) + d(a_lo, b_hi))
    return lax.dot_general(a, b, dn, preferred_element_type=F32, precision=precision)


def _dot(a, b, precision=None):
    return _contract(a, b, ((1,), (0,)), precision)


def _dot_nt(a, b, precision=None):
    return _contract(a, b, ((1,), (1,)), precision)


def _dot_tn(a, b, precision=None):
    return _contract(a, b, ((0,), (0,)), precision)


def _rms(x, g):
    return x * lax.rsqrt(jnp.mean(x * x, axis=-1, keepdims=True) + NORM_EPS) * g


def _ada_kernel(c_ref, w_ref, b_ref, o_ref):
    c = c_ref[...]
    cs = c * _sigmoid(c)
    o_ref[0] = _dot(cs, w_ref[0], HIGHEST) + b_ref[0]


def _ada_call(c_all, w_ada, b_ada):
    L, _, n6 = w_ada.shape
    nb = c_all.shape[0]
    tn = 1024
    return pl.pallas_call(
        _ada_kernel,
        grid=(L, n6 // tn),
        in_specs=[
            pl.BlockSpec((nb, D_MODEL), lambda l, j: (0, 0)),
            pl.BlockSpec((1, D_MODEL, tn), lambda l, j: (l, 0, j)),
            pl.BlockSpec((1, 1, tn), lambda l, j: (l, 0, j)),
        ],
        out_specs=pl.BlockSpec((1, nb, tn), lambda l, j: (l, 0, j)),
        out_shape=jax.ShapeDtypeStruct((L, nb, n6), F32),
        compiler_params=pltpu.CompilerParams(
            dimension_semantics=("arbitrary", "arbitrary"), vmem_limit_bytes=VMEM_LIMIT),
        name="ada_mod",
    )(c_all, w_ada, b_ada.reshape(L, 1, n6))


def _inproj_kernel(tb_ref, x_ref, mod_ref, g_ref, w_ref, wl_ref, proj_ref, p_ref, h_scr):
    del tb_ref

    @pl.when(pl.program_id(1) == 0)
    def _():
        h = _rms(x_ref[...], g_ref[...]) * (1.0 + mod_ref[0, 1:2, :]) + mod_ref[0, 0:1, :]
        hb = h.astype(BF16)
        h_scr[...] = hb
        p_ref[...] = _dot(hb, wl_ref[...])

    proj_ref[...] = _dot(h_scr[...], w_ref[...])


def _inproj_call(x, mod_l, g_pre, w_in_bf, wl_bf, tile_batch, tm):
    n_tok = x.shape[0]
    tn = 1024
    grid_spec = pltpu.PrefetchScalarGridSpec(
        num_scalar_prefetch=1,
        grid=(n_tok // tm, IN_WIDTH // tn),
        in_specs=[
            pl.BlockSpec((tm, D_MODEL), lambda i, n, tb: (i, 0)),
            pl.BlockSpec((1, 6, D_MODEL), lambda i, n, tb: (tb[i], 0, 0)),
            pl.BlockSpec((1, D_MODEL), lambda i, n, tb: (0, 0)),
            pl.BlockSpec((D_MODEL, tn), lambda i, n, tb: (0, n)),
            pl.BlockSpec((D_MODEL, LORA_ALL), lambda i, n, tb: (0, 0)),
        ],
        out_specs=[
            pl.BlockSpec((tm, tn), lambda i, n, tb: (i, n)),
            pl.BlockSpec((tm, LORA_ALL), lambda i, n, tb: (i, 0)),
        ],
        scratch_shapes=[pltpu.VMEM((tm, D_MODEL), BF16)],
    )
    return pl.pallas_call(
        _inproj_kernel,
        grid_spec=grid_spec,
        out_shape=[jax.ShapeDtypeStruct((n_tok, IN_WIDTH), F32),
                   jax.ShapeDtypeStruct((n_tok, LORA_ALL), F32)],
        compiler_params=pltpu.CompilerParams(
            dimension_semantics=("arbitrary", "arbitrary"), vmem_limit_bytes=VMEM_LIMIT),
        name="in_proj",
    )(tile_batch, x, mod_l, g_pre, w_in_bf, wl_bf)


def _att_kernel(q_ref, k_ref, v_ref, bias_ref, o_ref, kb_ref, vb_ref, *, rows, qrows):
    i = pl.program_id(2)

    @pl.when(i == 0)
    def _():
        kb_ref[...] = k_ref[...].astype(BF16)
        vb_ref[...] = v_ref[...].astype(BF16)

    scale = ATT_HEAD_DIM ** -0.5
    win = NA_ROWS * GRID_W

    def body(rr, carry):
        r = i * qrows + rr
        rs = jnp.clip(r - NA_ROWS // 2, 0, rows - NA_ROWS)
        off = r - rs
        q = q_ref[pl.ds(pl.multiple_of(rr * GRID_W, GRID_W), GRID_W), :].astype(BF16)
        start = pl.multiple_of(rs * GRID_W, GRID_W)
        kw = kb_ref[pl.ds(start, win), :]
        vw = vb_ref[pl.ds(start, win), :]
        s = _dot_nt(q, kw) * scale + bias_ref[0, off]
        m = jnp.max(s, axis=-1, keepdims=True)
        p = jnp.exp(s - m)
        l = jnp.sum(p, axis=-1, keepdims=True)
        pv = _dot(p.astype(BF16), vw)
        o_ref[pl.ds(pl.multiple_of(rr * GRID_W, GRID_W), GRID_W), :] = pv / l
        return carry

    lax.fori_loop(0, qrows, body, 0)


def _att_call(proj, bias_tab, tok_base, n_seq, seq_len, att_out_shape):
    rows = seq_len // GRID_W
    assert rows >= NA_ROWS and seq_len % GRID_W == 0
    qrows = 8
    qtok = qrows * GRID_W
    assert tok_base % seq_len == 0 and seq_len % qtok == 0
    seq0 = tok_base // seq_len
    qblk0 = tok_base // qtok
    nq = seq_len // qtok
    n_tok = n_seq * seq_len
    kern = functools.partial(_att_kernel, rows=rows, qrows=qrows)
    return pl.pallas_call(
        kern,
        grid=(n_seq, ATT_HEADS, nq),
        in_specs=[
            pl.BlockSpec((qtok, ATT_HEAD_DIM), lambda b, h, i: (qblk0 + b * nq + i, h)),
            pl.BlockSpec((seq_len, ATT_HEAD_DIM), lambda b, h, i: (seq0 + b, ATT_HEADS + h)),
            pl.BlockSpec((seq_len, ATT_HEAD_DIM), lambda b, h, i: (seq0 + b, 2 * ATT_HEADS + h)),
            pl.BlockSpec((1, NA_ROWS, GRID_W, NA_ROWS * GRID_W), lambda b, h, i: (h, 0, 0, 0)),
        ],
        out_specs=pl.BlockSpec((qtok, ATT_HEAD_DIM), lambda b, h, i: (b * nq + i, h)),
        out_shape=jax.ShapeDtypeStruct((n_tok, ATT_WIDTH), F32),
        scratch_shapes=[pltpu.VMEM((seq_len, ATT_HEAD_DIM), BF16),
                        pltpu.VMEM((seq_len, ATT_HEAD_DIM), BF16)],
        compiler_params=pltpu.CompilerParams(
            dimension_semantics=("arbitrary", "arbitrary", "arbitrary"), vmem_limit_bytes=VMEM_LIMIT),
        name="nbr_attention",
    )(proj, proj, proj, bias_tab)


def _att_bias_table(rpb_l):
    rows = jnp.stack([rpb_l[:, NA_ROWS - 1 - o:2 * NA_ROWS - 1 - o, :] for o in range(NA_ROWS)], axis=1)
    period = 2 * GRID_W
    vec = jnp.zeros(rows.shape[:-1] + (period,), F32)
    vec = vec.at[..., 0:NA_COLS].set(rows[..., NA_COLS - 1:])
    vec = vec.at[..., period - (NA_COLS - 1):].set(rows[..., :NA_COLS - 1])
    flat = jnp.tile(vec, (1, 1, 1, GRID_W))[..., :GRID_W * (period - 1)]
    toep = flat.reshape(ATT_HEADS, NA_ROWS, NA_ROWS, GRID_W, period - 1)[..., :GRID_W]
    tab = jnp.transpose(toep, (0, 1, 3, 2, 4))
    qc = np.arange(GRID_W)[:, None, None]
    kc = np.arange(GRID_W)[None, None, :]
    cs = np.clip(qc - NA_COLS // 2, 0, GRID_W - NA_COLS)
    valid = (kc >= cs) & (kc < cs + NA_COLS)
    tab = jnp.where(valid[None, None], tab, MASK_NEG)
    return tab.reshape(ATT_HEADS, NA_ROWS, GRID_W, NA_ROWS * GRID_W).astype(F32)


_PV_MU_R, _PV_MU_K, _PV_MU_V, _PV_W0, _PV_A0, _PV_V0, _PV_KK, _PV_KA = 0, 2, 4, 6, 8, 10, 11, 12
_PV_ROWS = 16


def _shift_prev(u, halo_row, row_id):
    return jnp.where(row_id == 0, halo_row, pltpu.roll(u, 1, axis=0))


def _shift_next(u, halo_row, row_id):
    n = u.shape[0]
    return jnp.where(row_id == n - 1, halo_row, pltpu.roll(u, n - 1, axis=0))


def _prep_kernel(first_ref, last_ref, r_ref, k_ref, v_ref, hp_ref, hn_ref, p_ref, php_ref, phn_ref,
                 pv_ref, w2_ref, *rest, has_vres):
    if has_vres:
        vf_ref, outs = rest[0], rest[1:]
    else:
        vf_ref, outs = None, rest
    i = pl.program_id(0)
    tm = r_ref.shape[0]
    keep_p = 1.0 - first_ref[i].astype(F32)
    keep_n = 1.0 - last_ref[i].astype(F32)
    row_id = lax.broadcasted_iota(jnp.int32, (tm, 1), 0)
    hp = hp_ref[7:8, :] * keep_p
    hn = hn_ref[0:1, :] * keep_n
    W = RWKV_WIDTH

    def pvec(row):
        return pv_ref[row:row + 1, :]

    def tshift(u, halo_p, halo_n, mu_row):
        prev = _shift_prev(u, halo_p, row_id)
        nxt = _shift_next(u, halo_n, row_id)
        return u + pvec(mu_row) * (prev - u) + pvec(mu_row + 1) * (nxt - u)

    r = tshift(r_ref[...], hp[:, 0:W], hn[:, 0:W], _PV_MU_R)
    k = tshift(k_ref[...], hp[:, W:2 * W], hn[:, W:2 * W], _PV_MU_K)
    v = tshift(v_ref[...], hp[:, 2 * W:3 * W], hn[:, 2 * W:3 * W], _PV_MU_V)

    C = LORA_COLS
    pall = p_ref[...]
    pre = (pall[:, 0:C]
           + _shift_prev(pall[:, C:2 * C], php_ref[7:8, C:2 * C] * keep_p, row_id)
           + _shift_next(pall[:, 2 * C:3 * C], phn_ref[0:1, 2 * C:3 * C] * keep_n, row_id))
    T = LORA_TILE
    tw = jnp.tanh(pre[:, 0:T])
    pa = pre[:, T:2 * T]
    pgv = pre[:, 2 * T:3 * T]

    if has_vres:
        vgate = _sigmoid(pvec(_PV_V0) + _dot(pgv, w2_ref[5], HIGHEST))
        v = v + (vf_ref[...] - v) * vgate
    gate = _dot(_sigmoid(pgv), w2_ref[4], HIGHEST)
    kku = k * pvec(_PV_KK)

    ri = lax.broadcasted_iota(jnp.int32, (tm, tm), 0)
    ci = lax.broadcasted_iota(jnp.int32, (tm, tm), 1)
    same = (ri // CHUNK) == (ci // CHUNK)
    tri = (jnp.where(same & (ci <= ri), 1.0, 0.0).astype(F32),
           jnp.where(same & (ci >= ri), 1.0, 0.0).astype(F32))

    per_dir = []
    for e in range(2):
        wlog = pvec(_PV_W0 + e) + _dot(tw, w2_ref[e], HIGHEST)
        lw = -math.exp(-0.5) * _sigmoid(wlog)
        a = _sigmoid(pvec(_PV_A0 + e) + _dot(pa, w2_ref[2 + e], HIGHEST))
        kd = k * (1.0 + (a - 1.0) * pvec(_PV_KA))
        cum = _dot(tri[e], lw, HIGHEST)
        per_dir.append((cum, kd, a))

    arrays = [r, v, kku, gate,
              per_dir[0][0], per_dir[0][1], per_dir[0][2],
              per_dir[1][0], per_dir[1][1], per_dir[1][2]]
    for arr, o_ref in zip(arrays, outs[:10]):
        for g in range(N_GROUPS):
            o_ref[g] = arr[:, g * GROUP_LANES:(g + 1) * GROUP_LANES]
    if not has_vres:
        outs[10][...] = v


def _prep_call(proj, p_lora, pvec, w2, v_first, first_flags, last_flags, tm):
    n_tok = proj.shape[0]
    has_vres = v_first is not None
    hb = tm // 8
    nb8 = n_tok // 8
    W = RWKV_WIDTH

    def cur(col):
        return lambda i, f, l: (i, col)

    def prev8(col):
        return lambda i, f, l: (jnp.maximum(i * hb - 1, 0), col)

    def next8(col):
        return lambda i, f, l: (jnp.minimum((i + 1) * hb, nb8 - 1), col)

    in_specs = [
        pl.BlockSpec((tm, W), cur(3)), pl.BlockSpec((tm, W), cur(4)), pl.BlockSpec((tm, W), cur(5)),
        pl.BlockSpec((8, 3 * W), prev8(1)), pl.BlockSpec((8, 3 * W), next8(1)),
        pl.BlockSpec((tm, LORA_ALL), cur(0)),
        pl.BlockSpec((8, LORA_ALL), prev8(0)), pl.BlockSpec((8, LORA_ALL), next8(0)),
        pl.BlockSpec((_PV_ROWS, W), lambda i, f, l: (0, 0)),
        pl.BlockSpec((6, LORA_TILE, W), lambda i, f, l: (0, 0, 0)),
    ]
    args = [proj, proj, proj, proj, proj, p_lora, p_lora, p_lora, pvec, w2]
    if has_vres:
        in_specs.append(pl.BlockSpec((tm, W), cur(0)))
        args.append(v_first)
    gshape = jax.ShapeDtypeStruct((N_GROUPS, n_tok, GROUP_LANES), F32)
    gspec = pl.BlockSpec((N_GROUPS, tm, GROUP_LANES), lambda i, f, l: (0, i, 0))
    out_shape = [gshape] * 10
    out_specs = [gspec] * 10
    if not has_vres:
        out_shape = out_shape + [jax.ShapeDtypeStruct((n_tok, W), F32)]
        out_specs = out_specs + [pl.BlockSpec((tm, W), cur(0))]
    grid_spec = pltpu.PrefetchScalarGridSpec(
        num_scalar_prefetch=2, grid=(n_tok // tm,), in_specs=in_specs, out_specs=out_specs)
    return pl.pallas_call(
        functools.partial(_prep_kernel, has_vres=has_vres),
        grid_spec=grid_spec,
        out_shape=out_shape,
        compiler_params=pltpu.CompilerParams(
            dimension_semantics=("arbitrary",), vmem_limit_bytes=VMEM_LIMIT),
        name="rwkv_prep",
    )(first_flags, last_flags, *args)


def _head_sum(x, m0):
    s0 = jnp.sum(jnp.where(m0, x, 0.0), axis=-1, keepdims=True)
    s1 = jnp.sum(jnp.where(m0, 0.0, x), axis=-1, keepdims=True)
    return jnp.where(m0, s0, s1)


def _stack_heads(z, m0):
    return jnp.concatenate([jnp.where(m0, z, 0.0), jnp.where(m0, 0.0, z)], axis=0)


def _wkv_kernel(blk_ref, first_ref, r_ref, v_ref, kku_ref, cum_ref, kd_ref, ad_ref, rk_ref, *rest,
                reverse, final):
    del blk_ref
    if final:
        yprev_ref, gate_ref, lnw_ref, lnb_ref, o_ref, state_ref = rest
    else:
        o_ref, state_ref = rest
    step = pl.program_id(0)
    C = CHUNK
    G = GROUP_LANES
    hd = RWKV_HEAD_DIM

    @pl.when(first_ref[step] == 1)
    def _():
        state_ref[...] = jnp.zeros_like(state_ref)

    lane = lax.broadcasted_iota(jnp.int32, (1, G), 1)
    m0 = lane < hd
    row = lax.broadcasted_iota(jnp.int32, (C, 1), 0)
    t_id = lax.broadcasted_iota(jnp.int32, (C, G), 0)
    s_id = lax.broadcasted_iota(jnp.int32, (C, G), 1) % hd
    if reverse:
        strict, incl = s_id > t_id, s_id >= t_id
    else:
        strict, incl = s_id < t_id, s_id <= t_id
    r2 = lax.broadcasted_iota(jnp.int32, (G, G), 0)
    c2 = lax.broadcasted_iota(jnp.int32, (G, G), 1)
    eye = (r2 == c2).astype(F32)
    same_head = (r2 // hd) == (c2 // hd)
    off_masks = [((r2 // (2 * b)) == (c2 // (2 * b))) & ((r2 // b) != (c2 // b)) for b in (1, 2, 4, 8, 16, 32)]
    hp = "bf16"
    hp_inv = "bf16"
    hp_state = "bf16x3"

    for g in range(N_GROUPS):
        r = r_ref[g]
        v = v_ref[g]
        kku = kku_ref[g]
        cum = cum_ref[g]
        kd = kd_ref[g]
        ad = ad_ref[g]
        kk = kku * lax.rsqrt(jnp.maximum(_head_sum(kku * kku, m0), 1e-24))
        a = -kk
        b = kk * ad
        if reverse:
            cum_x = jnp.where(row == C - 1, 0.0, pltpu.roll(cum, C - 1, axis=0))
            ltot = cum[0:1, :]
        else:
            cum_x = jnp.where(row == 0, 0.0, pltpu.roll(cum, 1, axis=0))
            ltot = cum[C - 1:C, :]
        e_neg = jnp.exp(-cum)
        e_tot = jnp.exp(ltot)
        at = a * jnp.exp(cum_x)
        rt = r * jnp.exp(cum)
        bt = b * e_neg
        kt = kd * e_neg
        bh = bt * e_tot
        kh = kt * e_tot

        lhs = jnp.concatenate([at, rt], axis=0)
        rhs = jnp.concatenate([_stack_heads(bt, m0), _stack_heads(kt, m0)], axis=0)
        q = _dot_nt(lhs, rhs, hp)
        a_ab = jnp.where(strict, q[:C, :G], 0.0)
        a_ak = jnp.where(strict, q[:C, G:], 0.0)
        a_rb = jnp.where(incl, q[C:, :G], 0.0)
        a_rk = jnp.where(incl, q[C:, G:], 0.0)

        x = _stack_heads(a_ab, m0)
        t_inv = eye + jnp.where(off_masks[0], x, 0.0)
        for off in off_masks[1:]:
            t_inv = t_inv + _dot(_dot(t_inv, jnp.where(off, x, 0.0), hp_inv), t_inv, hp_inv)
        t_cat = t_inv[:C] + t_inv[C:]

        vs = _stack_heads(v, m0)
        akv = _dot(a_ak, vs, hp)
        tz = _dot(t_cat, jnp.concatenate([_stack_heads(at, m0), _stack_heads(akv, m0)], axis=1), hp)
        atp = tz[:, :G]
        vp = tz[:, G:]
        w_cat = jnp.concatenate([a_rb, a_rk], axis=1)
        big = jnp.concatenate([
            jnp.concatenate([_stack_heads(atp, m0), _stack_heads(vp, m0)], axis=1),
            jnp.concatenate([jnp.zeros((G, G), F32), vs], axis=1)], axis=0)
        ry = _dot(w_cat, big, hp)
        rp = rt + ry[:, :G]
        yi = ry[:, G:]
        mt = jnp.where(same_head, _dot_tn(bh, atp, hp), 0.0) + eye * e_tot
        gt = jnp.where(same_head,
                       _dot_tn(jnp.concatenate([bh, kh], axis=0), jnp.concatenate([vp, v], axis=0), hp), 0.0)

        p_state = state_ref[g]
        y = _dot(rp, p_state, hp) + yi
        state_ref[g] = _dot(mt, p_state, hp_state) + gt
        y = y + _head_sum(r * kd * rk_ref[g], m0) * v
        if final:
            y = y + yprev_ref[g]
            mean = _head_sum(y, m0) * (1.0 / hd)
            dlt = y - mean
            var = _head_sum(dlt * dlt, m0) * (1.0 / hd)
            yn = dlt * lax.rsqrt(var + GN_EPS)
            o_ref[g] = (yn * lnw_ref[g] + lnb_ref[g]) * gate_ref[g]
        else:
            o_ref[g] = y


def _wkv_call(r, v, kku, cum, kd, ad, rk_g, blk, first, reverse, final_args=None):
    n_tok = r.shape[1]
    n_chunks = n_tok // CHUNK
    final = final_args is not None
    cspec = pl.BlockSpec((N_GROUPS, CHUNK, GROUP_LANES), lambda s, b, f: (0, b[s], 0))
    pspec = pl.BlockSpec((N_GROUPS, 1, GROUP_LANES), lambda s, b, f: (0, 0, 0))
    in_specs = [cspec] * 6 + [pspec]
    args = [r, v, kku, cum, kd, ad, rk_g]
    if final:
        y_prev, gate, lnw_g, lnb_g = final_args
        in_specs += [cspec, cspec, pspec, pspec]
        args += [y_prev, gate, lnw_g, lnb_g]
    grid_spec = pltpu.PrefetchScalarGridSpec(
        num_scalar_prefetch=2, grid=(n_chunks,), in_specs=in_specs, out_specs=cspec,
        scratch_shapes=[pltpu.VMEM((N_GROUPS, GROUP_LANES, GROUP_LANES), F32)])
    return pl.pallas_call(
        functools.partial(_wkv_kernel, reverse=reverse, final=final),
        grid_spec=grid_spec,
        out_shape=jax.ShapeDtypeStruct((N_GROUPS, n_tok, GROUP_LANES), F32),
        compiler_params=pltpu.CompilerParams(
            dimension_semantics=("arbitrary",), vmem_limit_bytes=VMEM_LIMIT),
        name="wkv_bwd" if reverse else "wkv_fwd",
    )(blk, first, *args)


def _outproj_kernel(tb_ref, x_ref, att_ref, rw_ref, mod_ref, gatt_ref, gpost_ref, w_ref, o_ref):
    del tb_ref
    att_n = _rms(att_ref[...], gatt_ref[...]).astype(BF16)
    rw = jnp.concatenate([rw_ref[g] for g in range(N_GROUPS)], axis=1).astype(BF16)
    mix = _dot(att_n, w_ref[0:ATT_WIDTH, :]) + _dot(rw, w_ref[ATT_WIDTH:, :])
    o_ref[...] = x_ref[...] + mod_ref[0, 2:3, :] * _rms(mix, gpost_ref[...])


def _outproj_call(x, att, rw_g, mod_l, g_att, g_post, w_out_bf, tile_batch, tm):
    n_tok = x.shape[0]
    grid_spec = pltpu.PrefetchScalarGridSpec(
        num_scalar_prefetch=1,
        grid=(n_tok // tm,),
        in_specs=[
            pl.BlockSpec((tm, D_MODEL), lambda i, tb: (i, 0)),
            pl.BlockSpec((tm, ATT_WIDTH), lambda i, tb: (i, 0)),
            pl.BlockSpec((N_GROUPS, tm, GROUP_LANES), lambda i, tb: (0, i, 0)),
            pl.BlockSpec((1, 6, D_MODEL), lambda i, tb: (tb[i], 0, 0)),
            pl.BlockSpec((1, ATT_WIDTH), lambda i, tb: (0, 0)),
            pl.BlockSpec((1, D_MODEL), lambda i, tb: (0, 0)),
            pl.BlockSpec((ATT_WIDTH + RWKV_WIDTH, D_MODEL), lambda i, tb: (0, 0)),
        ],
        out_specs=pl.BlockSpec((tm, D_MODEL), lambda i, tb: (i, 0)),
    )
    return pl.pallas_call(
        _outproj_kernel,
        grid_spec=grid_spec,
        out_shape=jax.ShapeDtypeStruct((n_tok, D_MODEL), F32),
        compiler_params=pltpu.CompilerParams(
            dimension_semantics=("arbitrary",), vmem_limit_bytes=VMEM_LIMIT),
        name="out_proj",
    )(tile_batch, x, att, rw_g, mod_l, g_att, g_post, w_out_bf)


def _ffn_kernel(tb_ref, x_ref, mod_ref, gpre_ref, gpost_ref, w1_ref, w2_ref, o_ref, h_scr, acc_scr):
    del tb_ref
    f = pl.program_id(1)

    @pl.when(f == 0)
    def _():
        h = _rms(x_ref[...], gpre_ref[...]) * (1.0 + mod_ref[0, 4:5, :]) + mod_ref[0, 3:4, :]
        h_scr[...] = h.astype(BF16)
        acc_scr[...] = jnp.zeros_like(acc_scr)

    u = jnp.maximum(_dot(h_scr[...], w1_ref[...]), 0.0)
    acc_scr[...] += _dot((u * u).astype(BF16), w2_ref[...])

    @pl.when(f == pl.num_programs(1) - 1)
    def _():
        o_ref[...] = x_ref[...] + mod_ref[0, 5:6, :] * _rms(acc_scr[...], gpost_ref[...])


def _ffn_call(x, mod_l, g_pre, g_post, w1_bf, w2_bf, tile_batch, tm):
    n_tok = x.shape[0]
    tf = 512
    grid_spec = pltpu.PrefetchScalarGridSpec(
        num_scalar_prefetch=1,
        grid=(n_tok // tm, D_FF // tf),
        in_specs=[
            pl.BlockSpec((tm, D_MODEL), lambda i, f, tb: (i, 0)),
            pl.BlockSpec((1, 6, D_MODEL), lambda i, f, tb: (tb[i], 0, 0)),
            pl.BlockSpec((1, D_MODEL), lambda i, f, tb: (0, 0)),
            pl.BlockSpec((1, D_MODEL), lambda i, f, tb: (0, 0)),
            pl.BlockSpec((D_MODEL, tf), lambda i, f, tb: (0, f)),
            pl.BlockSpec((tf, D_MODEL), lambda i, f, tb: (f, 0)),
        ],
        out_specs=pl.BlockSpec((tm, D_MODEL), lambda i, f, tb: (i, 0)),
        scratch_shapes=[pltpu.VMEM((tm, D_MODEL), BF16), pltpu.VMEM((tm, D_MODEL), F32)],
    )
    return pl.pallas_call(
        _ffn_kernel,
        grid_spec=grid_spec,
        out_shape=jax.ShapeDtypeStruct((n_tok, D_MODEL), F32),
        compiler_params=pltpu.CompilerParams(
            dimension_semantics=("arbitrary", "arbitrary"), vmem_limit_bytes=VMEM_LIMIT),
        name="ffn",
    )(tile_batch, x, mod_l, g_pre, g_post, w1_bf, w2_bf)


def _lora_in_weights(l, mu_x, w1, a1, g1, mu_v, v1):
    zeros32 = jnp.zeros((D_MODEL, 32), F32)
    if l == 0:
        mv = jnp.zeros((2, D_MODEL), F32)
        v1l = zeros32
    else:
        mv = mu_v[l - 1]
        v1l = v1[l - 1]
    mats = [(w1[l, 0], mu_x[l, 0]), (w1[l, 1], mu_x[l, 0]), (a1[l, 0], mu_x[l, 1]), (a1[l, 1], mu_x[l, 1]),
            (g1[l], mu_x[l, 2]), (v1l, mv), (zeros32, jnp.zeros((2, D_MODEL), F32))]
    groups = []
    for which in range(3):
        cols = []
        for w, mu in mats:
            if which == 0:
                s = 1.0 - mu[0] - mu[1]
            elif which == 1:
                s = mu[0]
            else:
                s = mu[1]
            cols.append(w * s[:, None])
        groups.append(jnp.concatenate(cols, axis=1))
    return jnp.concatenate(groups, axis=1).astype(BF16)


def _lora_out_weights(l, w2, a2, g2, v2):
    z64 = jnp.zeros((64, RWKV_WIDTH), F32)
    z32 = jnp.zeros((32, RWKV_WIDTH), F32)
    v2l = z32 if l == 0 else v2[l - 1]
    return jnp.stack([
        jnp.concatenate([w2[l, 0], z64], axis=0),
        jnp.concatenate([z64, w2[l, 1]], axis=0),
        jnp.concatenate([a2[l, 0], z64], axis=0),
        jnp.concatenate([z64, a2[l, 1]], axis=0),
        jnp.concatenate([g2[l], z64], axis=0),
        jnp.concatenate([z64, v2l, z32], axis=0),
    ])


def _channel_pack(l, mu_rkv, w0, a0, v0, k_k, k_a):
    z = jnp.zeros((1, RWKV_WIDTH), F32)
    v0l = z if l == 0 else v0[l - 1][None]
    rows = [mu_rkv[l].reshape(6, RWKV_WIDTH), w0[l], a0[l], v0l, k_k[l][None], k_a[l][None],
            jnp.zeros((_PV_ROWS - 13, RWKV_WIDTH), F32)]
    return jnp.concatenate(rows, axis=0)


def _to_groups(vec):
    return vec.reshape(N_GROUPS, 1, GROUP_LANES)


def _layout(trunks, tile):
    tb, first, last = [], [], []
    b0 = 0
    for n_seq, seq_len in trunks:
        per = seq_len // tile
        for b in range(n_seq):
            for t in range(per):
                tb.append(b0 + b)
                first.append(1 if t == 0 else 0)
                last.append(1 if t == per - 1 else 0)
        b0 += n_seq
    return (jnp.asarray(tb, jnp.int32), jnp.asarray(first, jnp.int32), jnp.asarray(last, jnp.int32))


def _forward(xs, cs, w_ada, b_ada, g_pre_mix, g_post_mix, g_pre_ffn, g_post_ffn, w_in, rpb, g_att_out,
             mu_rkv, mu_x, w0, w1, w2, a0, a1, a2, g1, g2, mu_v, v0, v1, v2, k_k, k_a, r_k,
             ln_x_w, ln_x_b, w_out, w_ffn1, w_ffn2, tm_big=512, tm_prep=256):
    trunks = [(x.shape[0], x.shape[1]) for x in xs]
    x = jnp.concatenate([xx.reshape(-1, D_MODEL) for xx in xs], axis=0)
    n_batch = sum(t[0] for t in trunks)
    nb_pad = -(-n_batch // 8) * 8
    c_all = jnp.concatenate(list(cs) + [jnp.zeros((nb_pad - n_batch, D_MODEL), F32)], axis=0)
    n_layers = w_in.shape[0]

    tb_big, _, _ = _layout(trunks, tm_big)
    _, first_prep, last_prep = _layout(trunks, tm_prep)
    _, first_chunk, last_chunk = _layout(trunks, CHUNK)
    n_chunks = first_chunk.shape[0]
    blk_fwd = jnp.arange(n_chunks, dtype=jnp.int32)
    blk_bwd = blk_fwd[::-1]
    first_bwd = last_chunk[::-1]

    mod = _ada_call(c_all, w_ada, b_ada).reshape(n_layers, nb_pad, 6, D_MODEL)

    v_first = None
    for l in range(n_layers):
        mod_l = mod[l]
        wl = _lora_in_weights(l, mu_x, w1, a1, g1, mu_v, v1)
        proj, p_lora = _inproj_call(x, mod_l, g_pre_mix[l][None], w_in[l].astype(BF16), wl, tb_big, tm_big)

        bias_tab = _att_bias_table(rpb[l])
        atts = []
        base = 0
        for n_seq, seq_len in trunks:
            atts.append(_att_call(proj, bias_tab, base, n_seq, seq_len, None))
            base += n_seq * seq_len
        att = jnp.concatenate(atts, axis=0)

        pvec = _channel_pack(l, mu_rkv, w0, a0, v0, k_k, k_a)
        w2pack = _lora_out_weights(l, w2, a2, g2, v2)
        outs = _prep_call(proj, p_lora, pvec, w2pack, v_first, first_prep, last_prep, tm_prep)
        r_g, v_g, kku_g, gate_g, cum0, kd0, ad0, cum1, kd1, ad1 = outs[:10]
        if l == 0:
            v_first = outs[10]
        rk_g = _to_groups(r_k[l].reshape(-1))
        y_f = _wkv_call(r_g, v_g, kku_g, cum0, kd0, ad0, rk_g, blk_fwd, first_chunk, reverse=False)
        rw_g = _wkv_call(r_g, v_g, kku_g, cum1, kd1, ad1, rk_g, blk_bwd, first_bwd, reverse=True,
                         final_args=(y_f, gate_g, _to_groups(ln_x_w[l]), _to_groups(ln_x_b[l])))

        x = _outproj_call(x, att, rw_g, mod_l, g_att_out[l][None], g_post_mix[l][None],
                          w_out[l].astype(BF16), tb_big, tm_big)
        x = _ffn_call(x, mod_l, g_pre_ffn[l][None], g_post_ffn[l][None],
                      w_ffn1[l].astype(BF16), w_ffn2[l].astype(BF16), tb_big, tm_big)

    outs = []
    base = 0
    for (n_seq, seq_len), xx in zip(trunks, xs):
        outs.append(x[base:base + n_seq * seq_len].reshape(xx.shape))
        base += n_seq * seq_len
    return tuple(outs)


def kernel(x_prompt, x_sample, c_prompt, c_sample, w_ada, b_ada, g_pre_mix, g_post_mix, g_pre_ffn, g_post_ffn, w_in, rpb, g_att_out, mu_rkv, mu_x, w0, w1, w2, a0, a1, a2, g1, g2, mu_v, v0, v1, v2, k_k, k_a, r_k, ln_x_w, ln_x_b, w_out, w_ffn1, w_ffn2):
    return _forward((x_prompt, x_sample), (c_prompt, c_sample), w_ada, b_ada, g_pre_mix, g_post_mix,
                    g_pre_ffn, g_post_ffn, w_in, rpb, g_att_out, mu_rkv, mu_x, w0, w1, w2, a0, a1, a2,
                    g1, g2, mu_v, v0, v1, v2, k_k, k_a, r_k, ln_x_w, ln_x_b, w_out, w_ffn1, w_ffn2)
```

```python
import functools
import math

import numpy as np
import jax
import jax.numpy as jnp
from jax import lax
from jax.experimental import pallas as pl
from jax.experimental.pallas import tpu as pltpu

F32 = jnp.float32
BF16 = jnp.bfloat16
HIGHEST = lax.Precision.HIGHEST

D_MODEL = 2048
DEPTH = 4
GRID_W = 64
ATT_WIDTH = 1024
RWKV_WIDTH = 1024
ATT_HEAD_DIM = 128
ATT_HEADS = 8
RWKV_HEAD_DIM = 64
NA_ROWS = 8
NA_COLS = 16
D_FF = 4 * D_MODEL
IN_WIDTH = 3 * ATT_WIDTH + 3 * RWKV_WIDTH
NORM_EPS = 1e-6
GN_EPS = 64e-5
LORA_TILE = 128
LORA_COLS = 3 * LORA_TILE
LORA_ALL = 3 * LORA_COLS
CHUNK = 64
GROUP_LANES = 128
N_GROUPS = RWKV_WIDTH // GROUP_LANES
MASK_NEG = -1e30
VMEM_LIMIT = 56 * 1024 * 1024


def _sigmoid(x):
    return 1.0 / (1.0 + jnp.exp(-x))


def _contract(a, b, dims, precision):
    dn = (dims, ((), ()))
    if precision == "bf16":
        return lax.dot_general(a.astype(BF16), b.astype(BF16), dn, preferred_element_type=F32)
    if precision == "bf16x3":
        a_hi = a.astype(BF16)
        a_lo = (a - a_hi.astype(F32)).astype(BF16)
        b_hi = b.astype(BF16)
        b_lo = (b - b_hi.astype(F32)).astype(BF16)

        def d(u, w):
            return lax.dot_general(u, w, dn, preferred_element_type=F32)

        return d(a_hi, b_hi) + (d(a_hi, b_lo) + d(a_lo, b_hi))
    return lax.dot_general(a, b, dn, preferred_element_type=F32, precision=precision)


def _dot(a, b, precision=None):
    return _contract(a, b, ((1,), (0,)), precision)


def _dot_nt(a, b, precision=None):
    return _contract(a, b, ((1,), (1,)), precision)


def _dot_tn(a, b, precision=None):
    return _contract(a, b, ((0,), (0,)), precision)


def _rms(x, g):
    return x * lax.rsqrt(jnp.mean(x * x, axis=-1, keepdims=True) + NORM_EPS) * g


def _ada_kernel(c_ref, w_ref, b_ref, o_ref):
    c = c_ref[...]
    cs = c * _sigmoid(c)
    o_ref[0] = _dot(cs, w_ref[0], HIGHEST) + b_ref[0]


def _ada_call(c_all, w_ada, b_ada):
    L, _, n6 = w_ada.shape
    nb = c_all.shape[0]
    tn = 1024
    return pl.pallas_call(
        _ada_kernel,
        grid=(L, n6 // tn),
        in_specs=[
            pl.BlockSpec((nb, D_MODEL), lambda l, j: (0, 0)),
            pl.BlockSpec((1, D_MODEL, tn), lambda l, j: (l, 0, j)),
            pl.BlockSpec((1, 1, tn), lambda l, j: (l, 0, j)),
        ],
        out_specs=pl.BlockSpec((1, nb, tn), lambda l, j: (l, 0, j)),
        out_shape=jax.ShapeDtypeStruct((L, nb, n6), F32),
        compiler_params=pltpu.CompilerParams(
            dimension_semantics=("arbitrary", "arbitrary"), vmem_limit_bytes=VMEM_LIMIT),
        name="ada_mod",
    )(c_all, w_ada, b_ada.reshape(L, 1, n6))


def _inproj_kernel(tb_ref, x_ref, mod_ref, g_ref, w_ref, wl_ref, proj_ref, p_ref, h_scr):
    del tb_ref

    @pl.when(pl.program_id(1) == 0)
    def _():
        h = _rms(x_ref[...], g_ref[...]) * (1.0 + mod_ref[0, 1:2, :]) + mod_ref[0, 0:1, :]
        hb = h.astype(BF16)
        h_scr[...] = hb
        p_ref[...] = _dot(hb, wl_ref[...])

    proj_ref[...] = _dot(h_scr[...], w_ref[...])


def _inproj_call(x, mod_l, g_pre, w_in_bf, wl_bf, tile_batch, tm):
    n_tok = x.shape[0]
    tn = 1024
    grid_spec = pltpu.PrefetchScalarGridSpec(
        num_scalar_prefetch=1,
        grid=(n_tok // tm, IN_WIDTH // tn),
        in_specs=[
            pl.BlockSpec((tm, D_MODEL), lambda i, n, tb: (i, 0)),
            pl.BlockSpec((1, 6, D_MODEL), lambda i, n, tb: (tb[i], 0, 0)),
            pl.BlockSpec((1, D_MODEL), lambda i, n, tb: (0, 0)),
            pl.BlockSpec((D_MODEL, tn), lambda i, n, tb: (0, n)),
            pl.BlockSpec((D_MODEL, LORA_ALL), lambda i, n, tb: (0, 0)),
        ],
        out_specs=[
            pl.BlockSpec((tm, tn), lambda i, n, tb: (i, n)),
            pl.BlockSpec((tm, LORA_ALL), lambda i, n, tb: (i, 0)),
        ],
        scratch_shapes=[pltpu.VMEM((tm, D_MODEL), BF16)],
    )
    return pl.pallas_call(
        _inproj_kernel,
        grid_spec=grid_spec,
        out_shape=[jax.ShapeDtypeStruct((n_tok, IN_WIDTH), F32),
                   jax.ShapeDtypeStruct((n_tok, LORA_ALL), F32)],
        compiler_params=pltpu.CompilerParams(
            dimension_semantics=("arbitrary", "arbitrary"), vmem_limit_bytes=VMEM_LIMIT),
        name="in_proj",
    )(tile_batch, x, mod_l, g_pre, w_in_bf, wl_bf)


def _att_kernel(q_ref, k_ref, v_ref, bias_ref, o_ref, kb_ref, vb_ref, *, rows, qrows):
    i = pl.program_id(2)

    @pl.when(i == 0)
    def _():
        kb_ref[...] = k_ref[...].astype(BF16)
        vb_ref[...] = v_ref[...].astype(BF16)

    scale = ATT_HEAD_DIM ** -0.5
    win = NA_ROWS * GRID_W

    def body(rr, carry):
        r = i * qrows + rr
        rs = jnp.clip(r - NA_ROWS // 2, 0, rows - NA_ROWS)
        off = r - rs
        q = q_ref[pl.ds(pl.multiple_of(rr * GRID_W, GRID_W), GRID_W), :].astype(BF16)
        start = pl.multiple_of(rs * GRID_W, GRID_W)
        kw = kb_ref[pl.ds(start, win), :]
        vw = vb_ref[pl.ds(start, win), :]
        s = _dot_nt(q, kw) * scale + bias_ref[0, off]
        m = jnp.max(s, axis=-1, keepdims=True)
        p = jnp.exp(s - m)
        l = jnp.sum(p, axis=-1, keepdims=True)
        pv = _dot(p.astype(BF16), vw)
        o_ref[pl.ds(pl.multiple_of(rr * GRID_W, GRID_W), GRID_W), :] = pv / l
        return carry

    lax.fori_loop(0, qrows, body, 0)


def _att_call(proj, bias_tab, tok_base, n_seq, seq_len, att_out_shape):
    rows = seq_len // GRID_W
    assert rows >= NA_ROWS and seq_len % GRID_W == 0
    qrows = 8
    qtok = qrows * GRID_W
    assert tok_base % seq_len == 0 and seq_len % qtok == 0
    seq0 = tok_base // seq_len
    qblk0 = tok_base // qtok
    nq = seq_len // qtok
    n_tok = n_seq * seq_len
    kern = functools.partial(_att_kernel, rows=rows, qrows=qrows)
    return pl.pallas_call(
        kern,
        grid=(n_seq, ATT_HEADS, nq),
        in_specs=[
            pl.BlockSpec((qtok, ATT_HEAD_DIM), lambda b, h, i: (qblk0 + b * nq + i, h)),
            pl.BlockSpec((seq_len, ATT_HEAD_DIM), lambda b, h, i: (seq0 + b, ATT_HEADS + h)),
            pl.BlockSpec((seq_len, ATT_HEAD_DIM), lambda b, h, i: (seq0 + b, 2 * ATT_HEADS + h)),
            pl.BlockSpec((1, NA_ROWS, GRID_W, NA_ROWS * GRID_W), lambda b, h, i: (h, 0, 0, 0)),
        ],
        out_specs=pl.BlockSpec((qtok, ATT_HEAD_DIM), lambda b, h, i: (b * nq + i, h)),
        out_shape=jax.ShapeDtypeStruct((n_tok, ATT_WIDTH), F32),
        scratch_shapes=[pltpu.VMEM((seq_len, ATT_HEAD_DIM), BF16),
                        pltpu.VMEM((seq_len, ATT_HEAD_DIM), BF16)],
        compiler_params=pltpu.CompilerParams(
            dimension_semantics=("arbitrary", "arbitrary", "arbitrary"), vmem_limit_bytes=VMEM_LIMIT),
        name="nbr_attention",
    )(proj, proj, proj, bias_tab)


def _att_bias_table(rpb_l):
    rows = jnp.stack([rpb_l[:, NA_ROWS - 1 - o:2 * NA_ROWS - 1 - o, :] for o in range(NA_ROWS)], axis=1)
    period = 2 * GRID_W
    vec = jnp.zeros(rows.shape[:-1] + (period,), F32)
    vec = vec.at[..., 0:NA_COLS].set(rows[..., NA_COLS - 1:])
    vec = vec.at[..., period - (NA_COLS - 1):].set(rows[..., :NA_COLS - 1])
    flat = jnp.tile(vec, (1, 1, 1, GRID_W))[..., :GRID_W * (period - 1)]
    toep = flat.reshape(ATT_HEADS, NA_ROWS, NA_ROWS, GRID_W, period - 1)[..., :GRID_W]
    tab = jnp.transpose(toep, (0, 1, 3, 2, 4))
    qc = np.arange(GRID_W)[:, None, None]
    kc = np.arange(GRID_W)[None, None, :]
    cs = np.clip(qc - NA_COLS // 2, 0, GRID_W - NA_COLS)
    valid = (kc >= cs) & (kc < cs + NA_COLS)
    tab = jnp.where(valid[None, None], tab, MASK_NEG)
    return tab.reshape(ATT_HEADS, NA_ROWS, GRID_W, NA_ROWS * GRID_W).astype(F32)


_PV_MU_R, _PV_MU_K, _PV_MU_V, _PV_W0, _PV_A0, _PV_V0, _PV_KK, _PV_KA = 0, 2, 4, 6, 8, 10, 11, 12
_PV_ROWS = 16


def _shift_prev(u, halo_row, row_id):
    return jnp.where(row_id == 0, halo_row, pltpu.roll(u, 1, axis=0))


def _shift_next(u, halo_row, row_id):
    n = u.shape[0]
    return jnp.where(row_id == n - 1, halo_row, pltpu.roll(u, n - 1, axis=0))


def _prep_kernel(first_ref, last_ref, r_ref, k_ref, v_ref, hp_ref, hn_ref, p_ref, php_ref, phn_ref,
                 pv_ref, w2_ref, *rest, has_vres):
    if has_vres:
        vf_ref, outs = rest[0], rest[1:]
    else:
        vf_ref, outs = None, rest
    i = pl.program_id(0)
    tm = r_ref.shape[0]
    keep_p = 1.0 - first_ref[i].astype(F32)
    keep_n = 1.0 - last_ref[i].astype(F32)
    row_id = lax.broadcasted_iota(jnp.int32, (tm, 1), 0)
    hp = hp_ref[7:8, :] * keep_p
    hn = hn_ref[0:1, :] * keep_n
    W = RWKV_WIDTH

    def pvec(row):
        return pv_ref[row:row + 1, :]

    def tshift(u, halo_p, halo_n, mu_row):
        prev = _shift_prev(u, halo_p, row_id)
        nxt = _shift_next(u, halo_n, row_id)
        return u + pvec(mu_row) * (prev - u) + pvec(mu_row + 1) * (nxt - u)

    r = tshift(r_ref[...], hp[:, 0:W], hn[:, 0:W], _PV_MU_R)
    k = tshift(k_ref[...], hp[:, W:2 * W], hn[:, W:2 * W], _PV_MU_K)
    v = tshift(v_ref[...], hp[:, 2 * W:3 * W], hn[:, 2 * W:3 * W], _PV_MU_V)

    C = LORA_COLS
    pall = p_ref[...]
    pre = (pall[:, 0:C]
           + _shift_prev(pall[:, C:2 * C], php_ref[7:8, C:2 * C] * keep_p, row_id)
           + _shift_next(pall[:, 2 * C:3 * C], phn_ref[0:1, 2 * C:3 * C] * keep_n, row_id))
    T = LORA_TILE
    tw = jnp.tanh(pre[:, 0:T])
    pa = pre[:, T:2 * T]
    pgv = pre[:, 2 * T:3 * T]

    if has_vres:
        vgate = _sigmoid(pvec(_PV_V0) + _dot(pgv, w2_ref[5], HIGHEST))
        v = v + (vf_ref[...] - v) * vgate
    gate = _dot(_sigmoid(pgv), w2_ref[4], HIGHEST)
    kku = k * pvec(_PV_KK)

    ri = lax.broadcasted_iota(jnp.int32, (tm, tm), 0)
    ci = lax.broadcasted_iota(jnp.int32, (tm, tm), 1)
    same = (ri // CHUNK) == (ci // CHUNK)
    tri = (jnp.where(same & (ci <= ri), 1.0, 0.0).astype(F32),
           jnp.where(same & (ci >= ri), 1.0, 0.0).astype(F32))

    per_dir = []
    for e in range(2):
        wlog = pvec(_PV_W0 + e) + _dot(tw, w2_ref[e], HIGHEST)
        lw = -math.exp(-0.5) * _sigmoid(wlog)
        a = _sigmoid(pvec(_PV_A0 + e) + _dot(pa, w2_ref[2 + e], HIGHEST))
        kd = k * (1.0 + (a - 1.0) * pvec(_PV_KA))
        cum = _dot(tri[e], lw, HIGHEST)
        per_dir.append((cum, kd, a))

    arrays = [r, v, kku, gate,
              per_dir[0][0], per_dir[0][1], per_dir[0][2],
              per_dir[1][0], per_dir[1][1], per_dir[1][2]]
    for arr, o_ref in zip(arrays, outs[:10]):
        for g in range(N_GROUPS):
            o_ref[g] = arr[:, g * GROUP_LANES:(g + 1) * GROUP_LANES]
    if not has_vres:
        outs[10][...] = v


def _prep_call(proj, p_lora, pvec, w2, v_first, first_flags, last_flags, tm):
    n_tok = proj.shape[0]
    has_vres = v_first is not None
    hb = tm // 8
    nb8 = n_tok // 8
    W = RWKV_WIDTH

    def cur(col):
        return lambda i, f, l: (i, col)

    def prev8(col):
        return lambda i, f, l: (jnp.maximum(i * hb - 1, 0), col)

    def next8(col):
        return lambda i, f, l: (jnp.minimum((i + 1) * hb, nb8 - 1), col)

    in_specs = [
        pl.BlockSpec((tm, W), cur(3)), pl.BlockSpec((tm, W), cur(4)), pl.BlockSpec((tm, W), cur(5)),
        pl.BlockSpec((8, 3 * W), prev8(1)), pl.BlockSpec((8, 3 * W), next8(1)),
        pl.BlockSpec((tm, LORA_ALL), cur(0)),
        pl.BlockSpec((8, LORA_ALL), prev8(0)), pl.BlockSpec((8, LORA_ALL), next8(0)),
        pl.BlockSpec((_PV_ROWS, W), lambda i, f, l: (0, 0)),
        pl.BlockSpec((6, LORA_TILE, W), lambda i, f, l: (0, 0, 0)),
    ]
    args = [proj, proj, proj, proj, proj, p_lora, p_lora, p_lora, pvec, w2]
    if has_vres:
        in_specs.append(pl.BlockSpec((tm, W), cur(0)))
        args.append(v_first)
    gshape = jax.ShapeDtypeStruct((N_GROUPS, n_tok, GROUP_LANES), F32)
    gspec = pl.BlockSpec((N_GROUPS, tm, GROUP_LANES), lambda i, f, l: (0, i, 0))
    out_shape = [gshape] * 10
    out_specs = [gspec] * 10
    if not has_vres:
        out_shape = out_shape + [jax.ShapeDtypeStruct((n_tok, W), F32)]
        out_specs = out_specs + [pl.BlockSpec((tm, W), cur(0))]
    grid_spec = pltpu.PrefetchScalarGridSpec(
        num_scalar_prefetch=2, grid=(n_tok // tm,), in_specs=in_specs, out_specs=out_specs)
    return pl.pallas_call(
        functools.partial(_prep_kernel, has_vres=has_vres),
        grid_spec=grid_spec,
        out_shape=out_shape,
        compiler_params=pltpu.CompilerParams(
            dimension_semantics=("arbitrary",), vmem_limit_bytes=VMEM_LIMIT),
        name="rwkv_prep",
    )(first_flags, last_flags, *args)


def _head_sum(x, m0):
    s0 = jnp.sum(jnp.where(m0, x, 0.0), axis=-1, keepdims=True)
    s1 = jnp.sum(jnp.where(m0, 0.0, x), axis=-1, keepdims=True)
    return jnp.where(m0, s0, s1)


def _stack_heads(z, m0):
    return jnp.concatenate([jnp.where(m0, z, 0.0), jnp.where(m0, 0.0, z)], axis=0)


def _wkv_kernel(blk_ref, first_ref, r_ref, v_ref, kku_ref, cum_ref, kd_ref, ad_ref, rk_ref, *rest,
                reverse, final):
    del blk_ref
    if final:
        yprev_ref, gate_ref, lnw_ref, lnb_ref, o_ref, state_ref = rest
    else:
        o_ref, state_ref = rest
    step = pl.program_id(0)
    C = CHUNK
    G = GROUP_LANES
    hd = RWKV_HEAD_DIM

    @pl.when(first_ref[step] == 1)
    def _():
        state_ref[...] = jnp.zeros_like(state_ref)

    lane = lax.broadcasted_iota(jnp.int32, (1, G), 1)
    m0 = lane < hd
    row = lax.broadcasted_iota(jnp.int32, (C, 1), 0)
    t_id = lax.broadcasted_iota(jnp.int32, (C, G), 0)
    s_id = lax.broadcasted_iota(jnp.int32, (C, G), 1) % hd
    if reverse:
        strict, incl = s_id > t_id, s_id >= t_id
    else:
        strict, incl = s_id < t_id, s_id <= t_id
    r2 = lax.broadcasted_iota(jnp.int32, (G, G), 0)
    c2 = lax.broadcasted_iota(jnp.int32, (G, G), 1)
    eye = (r2 == c2).astype(F32)
    same_head = (r2 // hd) == (c2 // hd)
    off_masks = [((r2 // (2 * b)) == (c2 // (2 * b))) & ((r2 // b) != (c2 // b)) for b in (1, 2, 4, 8, 16, 32)]
    hp = "bf16"
    hp_inv = "bf16"
    hp_state = "bf16x3"

    groups = range(N_GROUPS)
    pre = []
    for g in groups:
        r = r_ref[g]
        v = v_ref[g]
        kku = kku_ref[g]
        cum = cum_ref[g]
        kd = kd_ref[g]
        kk = kku * lax.rsqrt(jnp.maximum(_head_sum(kku * kku, m0), 1e-24))
        a = -kk
        b = kk * ad_ref[g]
        if reverse:
            cum_x = jnp.where(row == C - 1, 0.0, pltpu.roll(cum, C - 1, axis=0))
            ltot = cum[0:1, :]
        else:
            cum_x = jnp.where(row == 0, 0.0, pltpu.roll(cum, 1, axis=0))
            ltot = cum[C - 1:C, :]
        e_neg = jnp.exp(-cum)
        e_tot = jnp.exp(ltot)
        at = a * jnp.exp(cum_x)
        rt = r * jnp.exp(cum)
        bt = b * e_neg
        kt = kd * e_neg
        pre.append(dict(r=r, v=v, kd=kd, at=at, rt=rt, bt=bt, kt=kt, bh=bt * e_tot, kh=kt * e_tot, e_tot=e_tot,
                        vs=_stack_heads(v, m0)))

    qs = [_dot_nt(jnp.concatenate([p["at"], p["rt"]], axis=0),
                  jnp.concatenate([_stack_heads(p["bt"], m0), _stack_heads(p["kt"], m0)], axis=0), hp)
          for p in pre]
    a_ab = [jnp.where(strict, q[:C, :G], 0.0) for q in qs]
    a_ak = [jnp.where(strict, q[:C, G:], 0.0) for q in qs]
    w_cat = [jnp.concatenate([jnp.where(incl, q[C:, :G], 0.0), jnp.where(incl, q[C:, G:], 0.0)], axis=1)
             for q in qs]
    akv = [_dot(a_ak[g], pre[g]["vs"], hp) for g in groups]

    xs = [_stack_heads(a, m0) for a in a_ab]
    t_inv = [eye + jnp.where(off_masks[0], x, 0.0) for x in xs]
    for off in off_masks[1:]:
        tx = [_dot(t_inv[g], jnp.where(off, xs[g], 0.0), hp_inv) for g in groups]
        t_inv = [t_inv[g] + _dot(tx[g], t_inv[g], hp_inv) for g in groups]
    t_cat = [t[:C] + t[C:] for t in t_inv]

    tz = [_dot(t_cat[g], jnp.concatenate([_stack_heads(pre[g]["at"], m0), _stack_heads(akv[g], m0)], axis=1), hp)
          for g in groups]
    atp = [t[:, :G] for t in tz]
    vp = [t[:, G:] for t in tz]
    ry = [_dot(w_cat[g], jnp.concatenate([
        jnp.concatenate([_stack_heads(atp[g], m0), _stack_heads(vp[g], m0)], axis=1),
        jnp.concatenate([jnp.zeros((G, G), F32), pre[g]["vs"]], axis=1)], axis=0), hp)
          for g in groups]
    mt = [jnp.where(same_head, _dot_tn(pre[g]["bh"], atp[g], hp), 0.0) + eye * pre[g]["e_tot"] for g in groups]
    gt = [jnp.where(same_head, _dot_tn(jnp.concatenate([pre[g]["bh"], pre[g]["kh"]], axis=0),
                                       jnp.concatenate([vp[g], pre[g]["v"]], axis=0), hp), 0.0)
          for g in groups]
    p_state = [state_ref[g] for g in groups]
    ys = [_dot(pre[g]["rt"] + ry[g][:, :G], p_state[g], hp) + ry[g][:, G:] for g in groups]
    new_state = [_dot(mt[g], p_state[g], hp_state) + gt[g] for g in groups]
    for g in groups:
        state_ref[g] = new_state[g]
        p = pre[g]
        y = ys[g] + _head_sum(p["r"] * p["kd"] * rk_ref[g], m0) * p["v"]
        if final:
            y = y + yprev_ref[g]
            mean = _head_sum(y, m0) * (1.0 / hd)
            dlt = y - mean
            var = _head_sum(dlt * dlt, m0) * (1.0 / hd)
            yn = dlt * lax.rsqrt(var + GN_EPS)
            o_ref[g] = (yn * lnw_ref[g] + lnb_ref[g]) * gate_ref[g]
        else:
            o_ref[g] = y


def _wkv_call(r, v, kku, cum, kd, ad, rk_g, blk, first, reverse, final_args=None):
    n_tok = r.shape[1]
    n_chunks = n_tok // CHUNK
    final = final_args is not None
    cspec = pl.BlockSpec((N_GROUPS, CHUNK, GROUP_LANES), lambda s, b, f: (0, b[s], 0))
    pspec = pl.BlockSpec((N_GROUPS, 1, GROUP_LANES), lambda s, b, f: (0, 0, 0))
    in_specs = [cspec] * 6 + [pspec]
    args = [r, v, kku, cum, kd, ad, rk_g]
    if final:
        y_prev, gate, lnw_g, lnb_g = final_args
        in_specs += [cspec, cspec, pspec, pspec]
        args += [y_prev, gate, lnw_g, lnb_g]
    grid_spec = pltpu.PrefetchScalarGridSpec(
        num_scalar_prefetch=2, grid=(n_chunks,), in_specs=in_specs, out_specs=cspec,
        scratch_shapes=[pltpu.VMEM((N_GROUPS, GROUP_LANES, GROUP_LANES), F32)])
    return pl.pallas_call(
        functools.partial(_wkv_kernel, reverse=reverse, final=final),
        grid_spec=grid_spec,
        out_shape=jax.ShapeDtypeStruct((N_GROUPS, n_tok, GROUP_LANES), F32),
        compiler_params=pltpu.CompilerParams(
            dimension_semantics=("arbitrary",), vmem_limit_bytes=VMEM_LIMIT),
        name="wkv_bwd" if reverse else "wkv_fwd",
    )(blk, first, *args)


def _outproj_kernel(tb_ref, x_ref, att_ref, rw_ref, mod_ref, gatt_ref, gpost_ref, w_ref, o_ref):
    del tb_ref
    att_n = _rms(att_ref[...], gatt_ref[...]).astype(BF16)
    rw = jnp.concatenate([rw_ref[g] for g in range(N_GROUPS)], axis=1).astype(BF16)
    mix = _dot(att_n, w_ref[0:ATT_WIDTH, :]) + _dot(rw, w_ref[ATT_WIDTH:, :])
    o_ref[...] = x_ref[...] + mod_ref[0, 2:3, :] * _rms(mix, gpost_ref[...])


def _outproj_call(x, att, rw_g, mod_l, g_att, g_post, w_out_bf, tile_batch, tm):
    n_tok = x.shape[0]
    grid_spec = pltpu.PrefetchScalarGridSpec(
        num_scalar_prefetch=1,
        grid=(n_tok // tm,),
        in_specs=[
            pl.BlockSpec((tm, D_MODEL), lambda i, tb: (i, 0)),
            pl.BlockSpec((tm, ATT_WIDTH), lambda i, tb: (i, 0)),
            pl.BlockSpec((N_GROUPS, tm, GROUP_LANES), lambda i, tb: (0, i, 0)),
            pl.BlockSpec((1, 6, D_MODEL), lambda i, tb: (tb[i], 0, 0)),
            pl.BlockSpec((1, ATT_WIDTH), lambda i, tb: (0, 0)),
            pl.BlockSpec((1, D_MODEL), lambda i, tb: (0, 0)),
            pl.BlockSpec((ATT_WIDTH + RWKV_WIDTH, D_MODEL), lambda i, tb: (0, 0)),
        ],
        out_specs=pl.BlockSpec((tm, D_MODEL), lambda i, tb: (i, 0)),
    )
    return pl.pallas_call(
        _outproj_kernel,
        grid_spec=grid_spec,
        out_shape=jax.ShapeDtypeStruct((n_tok, D_MODEL), F32),
        compiler_params=pltpu.CompilerParams(
            dimension_semantics=("arbitrary",), vmem_limit_bytes=VMEM_LIMIT),
        name="out_proj",
    )(tile_batch, x, att, rw_g, mod_l, g_att, g_post, w_out_bf)


def _ffn_kernel(tb_ref, x_ref, mod_ref, gpre_ref, gpost_ref, w1_ref, w2_ref, o_ref, h_scr, acc_scr):
    del tb_ref
    f = pl.program_id(1)

    @pl.when(f == 0)
    def _():
        h = _rms(x_ref[...], gpre_ref[...]) * (1.0 + mod_ref[0, 4:5, :]) + mod_ref[0, 3:4, :]
        h_scr[...] = h.astype(BF16)
        acc_scr[...] = jnp.zeros_like(acc_scr)

    u = jnp.maximum(_dot(h_scr[...], w1_ref[...]), 0.0)
    acc_scr[...] += _dot((u * u).astype(BF16), w2_ref[...])

    @pl.when(f == pl.num_programs(1) - 1)
    def _():
        o_ref[...] = x_ref[...] + mod_ref[0, 5:6, :] * _rms(acc_scr[...], gpost_ref[...])


def _ffn_call(x, mod_l, g_pre, g_post, w1_bf, w2_bf, tile_batch, tm):
    n_tok = x.shape[0]
    tf = 512
    grid_spec = pltpu.PrefetchScalarGridSpec(
        num_scalar_prefetch=1,
        grid=(n_tok // tm, D_FF // tf),
        in_specs=[
            pl.BlockSpec((tm, D_MODEL), lambda i, f, tb: (i, 0)),
            pl.BlockSpec((1, 6, D_MODEL), lambda i, f, tb: (tb[i], 0, 0)),
            pl.BlockSpec((1, D_MODEL), lambda i, f, tb: (0, 0)),
            pl.BlockSpec((1, D_MODEL), lambda i, f, tb: (0, 0)),
            pl.BlockSpec((D_MODEL, tf), lambda i, f, tb: (0, f)),
            pl.BlockSpec((tf, D_MODEL), lambda i, f, tb: (f, 0)),
        ],
        out_specs=pl.BlockSpec((tm, D_MODEL), lambda i, f, tb: (i, 0)),
        scratch_shapes=[pltpu.VMEM((tm, D_MODEL), BF16), pltpu.VMEM((tm, D_MODEL), F32)],
    )
    return pl.pallas_call(
        _ffn_kernel,
        grid_spec=grid_spec,
        out_shape=jax.ShapeDtypeStruct((n_tok, D_MODEL), F32),
        compiler_params=pltpu.CompilerParams(
            dimension_semantics=("arbitrary", "arbitrary"), vmem_limit_bytes=VMEM_LIMIT),
        name="ffn",
    )(tile_batch, x, mod_l, g_pre, g_post, w1_bf, w2_bf)


def _lora_in_weights(l, mu_x, w1, a1, g1, mu_v, v1):
    zeros32 = jnp.zeros((D_MODEL, 32), F32)
    if l == 0:
        mv = jnp.zeros((2, D_MODEL), F32)
        v1l = zeros32
    else:
        mv = mu_v[l - 1]
        v1l = v1[l - 1]
    mats = [(w1[l, 0], mu_x[l, 0]), (w1[l, 1], mu_x[l, 0]), (a1[l, 0], mu_x[l, 1]), (a1[l, 1], mu_x[l, 1]),
            (g1[l], mu_x[l, 2]), (v1l, mv), (zeros32, jnp.zeros((2, D_MODEL), F32))]
    groups = []
    for which in range(3):
        cols = []
        for w, mu in mats:
            if which == 0:
                s = 1.0 - mu[0] - mu[1]
            elif which == 1:
                s = mu[0]
            else:
                s = mu[1]
            cols.append(w * s[:, None])
        groups.append(jnp.concatenate(cols, axis=1))
    return jnp.concatenate(groups, axis=1).astype(BF16)


def _lora_out_weights(l, w2, a2, g2, v2):
    z64 = jnp.zeros((64, RWKV_WIDTH), F32)
    z32 = jnp.zeros((32, RWKV_WIDTH), F32)
    v2l = z32 if l == 0 else v2[l - 1]
    return jnp.stack([
        jnp.concatenate([w2[l, 0], z64], axis=0),
        jnp.concatenate([z64, w2[l, 1]], axis=0),
        jnp.concatenate([a2[l, 0], z64], axis=0),
        jnp.concatenate([z64, a2[l, 1]], axis=0),
        jnp.concatenate([g2[l], z64], axis=0),
        jnp.concatenate([z64, v2l, z32], axis=0),
    ])


def _channel_pack(l, mu_rkv, w0, a0, v0, k_k, k_a):
    z = jnp.zeros((1, RWKV_WIDTH), F32)
    v0l = z if l == 0 else v0[l - 1][None]
    rows = [mu_rkv[l].reshape(6, RWKV_WIDTH), w0[l], a0[l], v0l, k_k[l][None], k_a[l][None],
            jnp.zeros((_PV_ROWS - 13, RWKV_WIDTH), F32)]
    return jnp.concatenate(rows, axis=0)


def _to_groups(vec):
    return vec.reshape(N_GROUPS, 1, GROUP_LANES)


def _layout(trunks, tile):
    tb, first, last = [], [], []
    b0 = 0
    for n_seq, seq_len in trunks:
        per = seq_len // tile
        for b in range(n_seq):
            for t in range(per):
                tb.append(b0 + b)
                first.append(1 if t == 0 else 0)
                last.append(1 if t == per - 1 else 0)
        b0 += n_seq
    return (jnp.asarray(tb, jnp.int32), jnp.asarray(first, jnp.int32), jnp.asarray(last, jnp.int32))


def _forward(xs, cs, w_ada, b_ada, g_pre_mix, g_post_mix, g_pre_ffn, g_post_ffn, w_in, rpb, g_att_out,
             mu_rkv, mu_x, w0, w1, w2, a0, a1, a2, g1, g2, mu_v, v0, v1, v2, k_k, k_a, r_k,
             ln_x_w, ln_x_b, w_out, w_ffn1, w_ffn2, tm_big=512, tm_prep=256):
    trunks = [(x.shape[0], x.shape[1]) for x in xs]
    x = jnp.concatenate([xx.reshape(-1, D_MODEL) for xx in xs], axis=0)
    n_batch = sum(t[0] for t in trunks)
    nb_pad = -(-n_batch // 8) * 8
    c_all = jnp.concatenate(list(cs) + [jnp.zeros((nb_pad - n_batch, D_MODEL), F32)], axis=0)
    n_layers = w_in.shape[0]

    tb_big, _, _ = _layout(trunks, tm_big)
    _, first_prep, last_prep = _layout(trunks, tm_prep)
    _, first_chunk, last_chunk = _layout(trunks, CHUNK)
    n_chunks = first_chunk.shape[0]
    blk_fwd = jnp.arange(n_chunks, dtype=jnp.int32)
    blk_bwd = blk_fwd[::-1]
    first_bwd = last_chunk[::-1]

    mod = _ada_call(c_all, w_ada, b_ada).reshape(n_layers, nb_pad, 6, D_MODEL)

    v_first = None
    for l in range(n_layers):
        mod_l = mod[l]
        wl = _lora_in_weights(l, mu_x, w1, a1, g1, mu_v, v1)
        proj, p_lora = _inproj_call(x, mod_l, g_pre_mix[l][None], w_in[l].astype(BF16), wl, tb_big, tm_big)

        bias_tab = _att_bias_table(rpb[l])
        atts = []
        base = 0
        for n_seq, seq_len in trunks:
            atts.append(_att_call(proj, bias_tab, base, n_seq, seq_len, None))
            base += n_seq * seq_len
        att = jnp.concatenate(atts, axis=0)

        pvec = _channel_pack(l, mu_rkv, w0, a0, v0, k_k, k_a)
        w2pack = _lora_out_weights(l, w2, a2, g2, v2)
        outs = _prep_call(proj, p_lora, pvec, w2pack, v_first, first_prep, last_prep, tm_prep)
        r_g, v_g, kku_g, gate_g, cum0, kd0, ad0, cum1, kd1, ad1 = outs[:10]
        if l == 0:
            v_first = outs[10]
        rk_g = _to_groups(r_k[l].reshape(-1))
        y_f = _wkv_call(r_g, v_g, kku_g, cum0, kd0, ad0, rk_g, blk_fwd, first_chunk, reverse=False)
        rw_g = _wkv_call(r_g, v_g, kku_g, cum1, kd1, ad1, rk_g, blk_bwd, first_bwd, reverse=True,
                         final_args=(y_f, gate_g, _to_groups(ln_x_w[l]), _to_groups(ln_x_b[l])))

        x = _outproj_call(x, att, rw_g, mod_l, g_att_out[l][None], g_post_mix[l][None],
                          w_out[l].astype(BF16), tb_big, tm_big)
        x = _ffn_call(x, mod_l, g_pre_ffn[l][None], g_post_ffn[l][None],
                      w_ffn1[l].astype(BF16), w_ffn2[l].astype(BF16), tb_big, tm_big)

    outs = []
    base = 0
    for (n_seq, seq_len), xx in zip(trunks, xs):
        outs.append(x[base:base + n_seq * seq_len].reshape(xx.shape))
        base += n_seq * seq_len
    return tuple(outs)


def kernel(x_prompt, x_sample, c_prompt, c_sample, w_ada, b_ada, g_pre_mix, g_post_mix, g_pre_ffn, g_post_ffn, w_in, rpb, g_att_out, mu_rkv, mu_x, w0, w1, w2, a0, a1, a2, g1, g2, mu_v, v0, v1, v2, k_k, k_a, r_k, ln_x_w, ln_x_b, w_out, w_ffn1, w_ffn2):
    return _forward((x_prompt, x_sample), (c_prompt, c_sample), w_ada, b_ada, g_pre_mix, g_post_mix,
                    g_pre_ffn, g_post_ffn, w_in, rpb, g_att_out, mu_rkv, mu_x, w0, w1, w2, a0, a1, a2,
                    g1, g2, mu_v, v0, v1, v2, k_k, k_a, r_k, ln_x_w, ln_x_b, w_out, w_ffn1, w_ffn2)
```
